```python
import math
import jax, jax.numpy as jnp
from jax import lax
import numpy as np

D_MODEL = 2048
BATCH = 2
SEQ = 8192
DEPTH = 4

HGRN_WIDTH = D_MODEL // 2
HGRN_HEAD_DIM = 128
HGRN_HEADS = HGRN_WIDTH // HGRN_HEAD_DIM
HGRN_CHUNK = 64
ATTN_HEAD_DIM = 64
ATTN_HEADS = (D_MODEL // 2) // ATTN_HEAD_DIM
ATTN_KV_HEADS = 4
ATTN_WIDTH = ATTN_HEADS * ATTN_HEAD_DIM
KV_WIDTH = ATTN_KV_HEADS * ATTN_HEAD_DIM
WINDOW = 128
CONV_WIDTH = D_MODEL // 2
CONV_K = 3
N_BUCKETS = 32
MAX_DISTANCE = 128
ALPHA = (2.0 * DEPTH) ** 0.25
BETA = (8.0 * DEPTH) ** -0.25
LN_EPS = 1e-5
RMS_EPS = 1e-6
SPLIT_SIZES = (
    HGRN_WIDTH, HGRN_WIDTH, HGRN_WIDTH, HGRN_WIDTH,
    ATTN_WIDTH, KV_WIDTH, KV_WIDTH, ATTN_WIDTH,
    CONV_WIDTH, CONV_WIDTH, CONV_WIDTH, CONV_WIDTH,
    D_MODEL, D_MODEL, D_MODEL,
)
N_IN = sum(SPLIT_SIZES)

kernel_name = "hybrid_hgrn2_swa_sink_shortconv_gated_merge"


def layer_norm(x, g, b):
    xf = x.astype(jnp.float32)
    mu = jnp.mean(xf, axis=-1, keepdims=True)
    var = jnp.mean(jnp.square(xf - mu), axis=-1, keepdims=True)
    return ((xf - mu) * lax.rsqrt(var + LN_EPS) * g.astype(jnp.float32) + b.astype(jnp.float32)).astype(x.dtype)


def t5_bucket(dist):
    max_exact = N_BUCKETS // 2
    is_small = dist < max_exact
    logd = jnp.log(jnp.maximum(dist, 1).astype(jnp.float32) / max_exact) / math.log(MAX_DISTANCE / max_exact)
    large = max_exact + (logd * (N_BUCKETS - max_exact)).astype(jnp.int32)
    large = jnp.minimum(large, N_BUCKETS - 1)
    return jnp.where(is_small, dist, large)


def band_relative_bias(rel_bias):
    i = jnp.arange(WINDOW)[:, None]
    j = jnp.arange(2 * WINDOW)[None, :]
    rel = jnp.clip(WINDOW + i - j, 0, WINDOW - 1)
    bucket = t5_bucket(rel)
    return jnp.transpose(rel_bias[bucket], (2, 0, 1)).astype(jnp.float32)


def hgrn2_mixer(q, f_logit, inp, lb):
    bsz, t, _ = q.shape
    h, d, c = HGRN_HEADS, HGRN_HEAD_DIM, HGRN_CHUNK
    nc = t // c
    qf = jax.nn.silu(q.astype(jnp.float32)) * (d ** -0.5)
    f = lb + (1.0 - lb) * jax.nn.sigmoid(f_logit.astype(jnp.float32))
    k = 1.0 - f
    g = jnp.log(f)
    v = inp.astype(jnp.float32)

    def to_chunks(a):
        return jnp.transpose(a.reshape(bsz, nc, c, h, d), (1, 0, 3, 2, 4))

    causal = jnp.tril(jnp.ones((c, c), dtype=bool))

    def step(state, chunk):
        qc, kc, vc, gc = chunk
        b = jnp.cumsum(gc, axis=2)
        inter = jnp.einsum('bhtk,bhkv->bhtv', qc * jnp.exp(b), state)
        diff = b[:, :, :, None, :] - b[:, :, None, :, :]
        decay = jnp.exp(jnp.where(causal[None, None, :, :, None], diff, -jnp.inf))
        scores = jnp.einsum('bhtk,bhsk,bhtsk->bhts', qc, kc, decay)
        intra = jnp.einsum('bhts,bhsv->bhtv', scores, vc)
        b_end = b[:, :, -1, :]
        k_to_end = kc * jnp.exp(b_end[:, :, None, :] - b)
        new_state = jnp.exp(b_end)[..., None] * state + jnp.einsum('bhsk,bhsv->bhkv', k_to_end, vc)
        return new_state, inter + intra

    s0 = jnp.zeros((bsz, h, d, d), jnp.float32)
    _, o = lax.scan(step, s0, (to_chunks(qf), to_chunks(k), to_chunks(v), to_chunks(g)))
    return jnp.transpose(o, (1, 0, 3, 2, 4)).reshape(bsz, t, h, d)


def swa_sink_attention(q, k, v, bias, sinks):
    bsz, t, _ = q.shape
    w, dh, kvh = WINDOW, ATTN_HEAD_DIM, ATTN_KV_HEADS
    grp = ATTN_HEADS // kvh
    nb = t // w
    qb = q.reshape(bsz, nb, w, kvh, grp, dh)
    kb = k.reshape(bsz, nb, w, kvh, dh)
    vb = v.reshape(bsz, nb, w, kvh, dh)

    def with_prev(a):
        prev = jnp.concatenate([jnp.zeros_like(a[:, :1]), a[:, :-1]], axis=1)
        return jnp.concatenate([prev, a], axis=2)

    kw, vw = with_prev(kb), with_prev(vb)
    s = jnp.einsum('bnqhgd,bnkhd->bhgnqk', qb, kw).astype(jnp.float32) * (dh ** -0.5)
    s = s + bias.reshape(kvh, grp, 1, w, 2 * w)
    i = jnp.arange(w)[:, None]
    j = jnp.arange(2 * w)[None, :]
    rel = w + i - j
    band = (rel >= 0) & (rel < w)
    key_pos = jnp.arange(nb)[:, None, None] * w - w + j[None]
    mask = band[None] & (key_pos >= 0)
    s = jnp.where(mask, s, -jnp.inf)
    sink = sinks.astype(jnp.float32).reshape(kvh, grp, 1, 1, 1)
    m = jnp.maximum(jnp.max(s, axis=-1, keepdims=True), sink)
    p = jnp.exp(s - m)
    p = p / (jnp.sum(p, axis=-1, keepdims=True) + jnp.exp(sink - m))
    o = jnp.einsum('bhgnqk,bnkhd->bnqhgd', p.astype(v.dtype), vw)
    return o.reshape(bsz, t, ATTN_WIDTH)


def short_gated_conv(b_gate, c_gate, xin, conv_w):
    h = c_gate * xin
    hp = jnp.pad(h, ((0, 0), (CONV_K - 1, 0), (0, 0)))
    t = h.shape[1]
    y = conv_w[0] * hp[:, 0:t] + conv_w[1] * hp[:, 1:t + 1] + conv_w[2] * hp[:, 2:t + 2]
    return b_gate * y


def setup_inputs(seed: int = 0) -> dict:
    key = jax.random.key(seed)
    ks = jax.random.split(key, 13)
    nrm = jax.random.normal
    f32 = jnp.float32
    return {
        "x": nrm(ks[0], (BATCH, SEQ, D_MODEL), f32),
        "w_in": nrm(ks[1], (DEPTH, D_MODEL, N_IN), f32) * D_MODEL ** -0.5,
        "w_proj_hgrn": nrm(ks[2], (DEPTH, HGRN_WIDTH, D_MODEL), f32) * (HGRN_WIDTH ** -0.5 * BETA),
        "w_proj_attn": nrm(ks[3], (DEPTH, ATTN_WIDTH, D_MODEL), f32) * (ATTN_WIDTH ** -0.5 * BETA),
        "w_proj_conv": nrm(ks[4], (DEPTH, CONV_WIDTH, D_MODEL), f32) * (CONV_WIDTH ** -0.5 * BETA),
        "w_out": nrm(ks[5], (DEPTH, D_MODEL, D_MODEL), f32) * (D_MODEL ** -0.5 * BETA),
        "lb_param": nrm(ks[6], (DEPTH, HGRN_WIDTH), f32) * 0.5,
        "hgrn_norm_g": 1.0 + 0.02 * nrm(ks[7], (DEPTH, HGRN_WIDTH), f32),
        "attn_sinks": nrm(ks[8], (DEPTH, ATTN_HEADS), f32),
        "conv_w": nrm(ks[9], (DEPTH, CONV_K, CONV_WIDTH), f32) * CONV_K ** -0.5,
        "rel_bias": nrm(ks[10], (N_BUCKETS, ATTN_HEADS), f32) * 0.1,
        "ln_g": 1.0 + 0.02 * nrm(ks[11], (DEPTH, D_MODEL), f32),
        "ln_b": 0.02 * nrm(ks[12], (DEPTH, D_MODEL), f32),
    }


def reference(x, w_in, w_proj_hgrn, w_proj_attn, w_proj_conv, w_out, lb_param, hgrn_norm_g,
              attn_sinks, conv_w, rel_bias, ln_g, ln_b):
    bsz, t, _ = x.shape
    split_idx = [int(s) for s in np.cumsum(SPLIT_SIZES)[:-1]]
    lb_soft = jax.nn.softmax(lb_param.astype(jnp.float32), axis=0)
    lower_bounds = jnp.cumsum(lb_soft, axis=0) - lb_soft[0:1]
    bias = band_relative_bias(rel_bias)
    for l in range(DEPTH):
        u = x @ w_in[l]
        (a_q, a_f, a_i, a_g, b_q, b_k, b_v, b_g,
         c_b, c_c, c_x, c_g, m_a, m_b, m_c) = jnp.split(u, split_idx, axis=-1)
        o_a = hgrn2_mixer(a_q, a_f, a_i, lower_bounds[l])
        o_a = o_a * lax.rsqrt(jnp.mean(jnp.square(o_a), axis=-1, keepdims=True) + RMS_EPS)
        o_a = o_a.reshape(bsz, t, HGRN_WIDTH) * hgrn_norm_g[l].astype(jnp.float32)
        y_a = (o_a.astype(x.dtype) * jax.nn.silu(a_g)) @ w_proj_hgrn[l]
        o_b = swa_sink_attention(b_q, b_k, b_v, bias, attn_sinks[l])
        y_b = (o_b * jax.nn.silu(b_g)) @ w_proj_attn[l]
        o_c = short_gated_conv(c_b, c_c, c_x, conv_w[l])
        y_c = (o_c * jax.nn.silu(c_g)) @ w_proj_conv[l]
        merged = jax.nn.sigmoid(m_a) * y_a + jax.nn.sigmoid(m_b) * y_b + jax.nn.sigmoid(m_c) * y_c
        y = merged @ w_out[l]
        x = layer_norm(ALPHA * x + y, ln_g[l], ln_b[l])
    return x
```

```python
import functools
import math

import numpy as np
import jax
import jax.numpy as jnp
from jax import lax
from jax.experimental import pallas as pl
from jax.experimental.pallas import tpu as pltpu

D_MODEL = 2048
DEPTH = 4
HGRN_WIDTH = 1024
HGRN_HEAD_DIM = 128
HGRN_HEADS = HGRN_WIDTH // HGRN_HEAD_DIM
ATTN_HEAD_DIM = 64
ATTN_HEADS = 16
ATTN_KV_HEADS = 4
ATTN_GROUP = ATTN_HEADS // ATTN_KV_HEADS
ATTN_WIDTH = ATTN_HEADS * ATTN_HEAD_DIM
KV_WIDTH = ATTN_KV_HEADS * ATTN_HEAD_DIM
WINDOW = 128
CONV_WIDTH = 1024
CONV_K = 3
N_BUCKETS = 32
MAX_DISTANCE = 128
ALPHA = (2.0 * DEPTH) ** 0.25
LN_EPS = 1e-5
RMS_EPS = 1e-6
NEG_BIG = -1e30

_REF_SEGMENTS = (
    ("a_q", 1024), ("a_f", 1024), ("a_i", 1024), ("a_g", 1024),
    ("b_q", 1024), ("b_k", 256), ("b_v", 256), ("b_g", 1024),
    ("c_b", 1024), ("c_c", 1024), ("c_x", 1024), ("c_g", 1024),
    ("m_a", 2048), ("m_b", 2048), ("m_c", 2048),
)
_OUR_ORDER = ("a_q", "a_f", "a_i", "a_g", "b_q", "b_g", "c_b", "c_c", "c_x", "c_g",
              "m_a", "m_b", "m_c", "b_k", "b_v")
N_IN = sum(w for _, w in _REF_SEGMENTS)


def _segment_offsets():
    ref_off, off = {}, 0
    for name, w in _REF_SEGMENTS:
        ref_off[name] = (off, w)
        off += w
    our_off, off = {}, 0
    for name in _OUR_ORDER:
        w = ref_off[name][1]
        assert off % w == 0
        our_off[name] = off
        off += w
    return ref_off, our_off


_REF_OFF, _OUR_OFF = _segment_offsets()

LANES = 128
VMEM_LIMIT_BYTES = 56 * 1024 * 1024

HGRN_CHUNK = 64
HGRN_SUB = 16


def _t5_bucket_table():
    i = np.arange(WINDOW)[:, None]
    j = np.arange(2 * WINDOW)[None, :]
    rel = np.clip(WINDOW + i - j, 0, WINDOW - 1)
    max_exact = N_BUCKETS // 2
    logd = (np.log(np.maximum(rel, 1).astype(np.float32) / np.float32(max_exact))
            / np.float32(math.log(MAX_DISTANCE / max_exact))).astype(np.float32)
    large = max_exact + (logd * np.float32(N_BUCKETS - max_exact)).astype(np.int32)
    large = np.minimum(large, N_BUCKETS - 1)
    return np.where(rel < max_exact, rel, large).astype(np.int32)


def _sigmoid(x):
    return 1.0 / (1.0 + jnp.exp(-x))


def _silu(x):
    return x * _sigmoid(x)


def _prep_kernel(lb_ref, relb_ref, bucket_ref, lower_ref, bias_ref):
    lb = lb_ref[...]
    m = jnp.max(lb, axis=0, keepdims=True)
    e = jnp.exp(lb - m)
    soft = e / jnp.sum(e, axis=0, keepdims=True)
    acc = jnp.zeros_like(soft[0:1])
    for l in range(DEPTH):
        acc = acc + soft[l:l + 1]
        lower_ref[l:l + 1, :] = acc - soft[0:1]
    bucket = bucket_ref[...]
    for h in range(ATTN_HEADS):
        acc_b = jnp.zeros((WINDOW, 2 * WINDOW), jnp.float32)
        for bk in range(N_BUCKETS):
            acc_b = jnp.where(bucket == bk, relb_ref[bk, h], acc_b)
        bias_ref[h] = acc_b


def _prep(lb_param, rel_bias):
    bucket = jnp.asarray(_t5_bucket_table())
    return pl.pallas_call(
        _prep_kernel,
        out_shape=(jax.ShapeDtypeStruct((DEPTH, HGRN_WIDTH), jnp.float32),
                   jax.ShapeDtypeStruct((ATTN_HEADS, WINDOW, 2 * WINDOW), jnp.float32)),
        in_specs=[pl.BlockSpec(memory_space=pltpu.VMEM),
                  pl.BlockSpec(memory_space=pltpu.SMEM),
                  pl.BlockSpec(memory_space=pltpu.VMEM)],
        out_specs=(pl.BlockSpec(memory_space=pltpu.VMEM),
                   pl.BlockSpec(memory_space=pltpu.VMEM)),
        name="prep",
    )(lb_param.astype(jnp.float32), rel_bias.astype(jnp.float32), bucket)


def _inproj_kernel(x_ref, w_ref, o_ref):
    o_ref[...] = jnp.dot(x_ref[...], w_ref[...], preferred_element_type=jnp.float32)


def _inproj(x_bf16, w_bf16, layer):
    m = x_bf16.shape[0]
    tm = min(1024, m)
    tn = 1536
    assert m % tm == 0 and N_IN % tn == 0
    return pl.pallas_call(
        _inproj_kernel,
        out_shape=jax.ShapeDtypeStruct((m, N_IN), jnp.float32),
        grid=(m // tm, N_IN // tn),
        in_specs=[pl.BlockSpec((tm, D_MODEL), lambda i, j: (i, 0)),
                  pl.BlockSpec((None, D_MODEL, tn), lambda i, j: (layer, 0, j))],
        out_specs=pl.BlockSpec((tm, tn), lambda i, j: (i, j)),
        compiler_params=pltpu.CompilerParams(
            dimension_semantics=("parallel", "arbitrary"), vmem_limit_bytes=VMEM_LIMIT_BYTES),
        name="inproj",
    )(x_bf16, w_bf16)


def _hgrn_chunk(q, fl, v, lb, state_t):
    c, s = HGRN_CHUNK, HGRN_SUB
    bf = jnp.bfloat16
    qf = _silu(q) * (HGRN_HEAD_DIM ** -0.5)
    f = lb + (1.0 - lb) * _sigmoid(fl)
    k = 1.0 - f
    g = jnp.log(f)
    row = lax.broadcasted_iota(jnp.int32, (c, c), 0)
    col = lax.broadcasted_iota(jnp.int32, (c, c), 1)
    tril = (col <= row).astype(bf)
    g1 = g.astype(bf)
    r1 = g - g1.astype(jnp.float32)
    g2 = r1.astype(bf)
    g3 = (r1 - g2.astype(jnp.float32)).astype(bf)
    b = (jnp.dot(tril, g1, preferred_element_type=jnp.float32)
         + jnp.dot(tril, g2, preferred_element_type=jnp.float32)
         + jnp.dot(tril, g3, preferred_element_type=jnp.float32))
    b_end = b[c - 1:c, :]
    qe = (qf * jnp.exp(b)).astype(bf)
    o_inter = lax.dot_general(qe, state_t.astype(bf), (((1,), (1,)), ((), ())),
                              preferred_element_type=jnp.float32)
    blocks = [jnp.zeros((s, c), jnp.float32)]
    for i in range(1, c // s):
        bref = b[s * i - 1:s * i, :]
        a_i = (qf[s * i:s * (i + 1)] * jnp.exp(b[s * i:s * (i + 1)] - bref)).astype(bf)
        b_i = (k * jnp.exp(jnp.minimum(bref - b, 0.0))).astype(bf)
        blocks.append(lax.dot_general(a_i, b_i, (((1,), (1,)), ((), ())),
                                      preferred_element_type=jnp.float32))
    scores = jnp.concatenate(blocks, axis=0)
    scores = jnp.where((col // s) < (row // s), scores, 0.0)
    o_off = jnp.dot(scores.astype(bf), v.astype(bf), preferred_element_type=jnp.float32)
    tsub = lax.broadcasted_iota(jnp.int32, (s, HGRN_HEAD_DIM), 0)
    diag = []
    for j in range(c // s):
        sl = slice(s * j, s * (j + 1))
        bb, qq, kk, vv = b[sl], qf[sl], k[sl], v[sl]
        acc = jnp.zeros((s, HGRN_HEAD_DIM), jnp.float32)
        for r in range(s):
            e = jnp.exp(jnp.where(tsub >= r, bb - bb[r:r + 1, :], NEG_BIG))
            w = jnp.sum(qq * kk[r:r + 1, :] * e, axis=-1, keepdims=True)
            acc = acc + w * vv[r:r + 1, :]
        diag.append(acc)
    o = o_inter + o_off + jnp.concatenate(diag, axis=0)
    kte = (k * jnp.exp(b_end - b)).astype(bf)
    upd = lax.dot_general(v.astype(bf), kte, (((0,), (0,)), ((), ())),
                          preferred_element_type=jnp.float32)
    new_state = state_t * jnp.exp(b_end) + upd
    return o, new_state


def _hgrn_kernel(q_ref, f_ref, i_ref, g_ref, lb_ref, ng_ref, o_ref, state_ref, *, n_chunks):
    @pl.when(pl.program_id(2) == 0)
    def _():
        state_ref[...] = jnp.zeros_like(state_ref)

    lb = lb_ref[0]
    ng = ng_ref[0]

    def body(ci, carry):
        rows = pl.ds(pl.multiple_of(ci * HGRN_CHUNK, HGRN_CHUNK), HGRN_CHUNK)
        o, new_state = _hgrn_chunk(q_ref[rows, :], f_ref[rows, :], i_ref[rows, :], lb, state_ref[...])
        state_ref[...] = new_state
        o = o * lax.rsqrt(jnp.mean(o * o, axis=-1, keepdims=True) + RMS_EPS) * ng
        o_ref[rows, :] = (o * _silu(g_ref[rows, :])).astype(o_ref.dtype)
        return carry

    lax.fori_loop(0, n_chunks, body, 0)


def _hgrn(u, lower_l, norm_g_l, bsz, t):
    tc = min(512, t)
    assert t % tc == 0 and tc % HGRN_CHUNK == 0
    nt = t // tc
    d = HGRN_HEAD_DIM

    def col_spec(name):
        base = _OUR_OFF[name] // d
        return pl.BlockSpec((tc, d), lambda b, h, i: (b * nt + i, base + h))

    vec_spec = pl.BlockSpec((1, 1, d), lambda b, h, i: (h, 0, 0))
    return pl.pallas_call(
        functools.partial(_hgrn_kernel, n_chunks=tc // HGRN_CHUNK),
        out_shape=jax.ShapeDtypeStruct((bsz * t, HGRN_WIDTH), jnp.bfloat16),
        grid=(bsz, HGRN_HEADS, nt),
        in_specs=[col_spec("a_q"), col_spec("a_f"), col_spec("a_i"), col_spec("a_g"),
                  vec_spec, vec_spec],
        out_specs=pl.BlockSpec((tc, d), lambda b, h, i: (b * nt + i, h)),
        scratch_shapes=[pltpu.VMEM((d, d), jnp.float32)],
        compiler_params=pltpu.CompilerParams(
            dimension_semantics=("parallel", "parallel", "arbitrary"),
            vmem_limit_bytes=VMEM_LIMIT_BYTES),
        name="hgrn",
    )(u, u, u, u, lower_l.reshape(HGRN_HEADS, 1, d), norm_g_l.reshape(HGRN_HEADS, 1, d))


def _attn_kernel(sink_ref, q_ref, kvp_ref, kvc_ref, g_ref, bias_ref, o_ref):
    w, dh = WINDOW, ATTN_HEAD_DIM
    bf = jnp.bfloat16
    first = pl.program_id(1) == 0
    i = lax.broadcasted_iota(jnp.int32, (w, 2 * w), 0)
    j = lax.broadcasted_iota(jnp.int32, (w, 2 * w), 1)
    rel = w + i - j
    mask = (rel >= 0) & (rel < w) & ((j >= w) | jnp.logical_not(first))
    kv_prev = kvp_ref[...]
    kv_cur = kvc_ref[...]
    outs = []
    for kvh in range(ATTN_KV_HEADS):
        ks = slice(kvh * dh, (kvh + 1) * dh)
        vs = slice(KV_WIDTH + kvh * dh, KV_WIDTH + (kvh + 1) * dh)
        kw = jnp.concatenate([kv_prev[:, ks], kv_cur[:, ks]], axis=0).astype(bf)
        vw = jnp.concatenate([kv_prev[:, vs], kv_cur[:, vs]], axis=0).astype(bf)
        for gi in range(ATTN_GROUP):
            h = kvh * ATTN_GROUP + gi
            q = q_ref[:, h * dh:(h + 1) * dh].astype(bf)
            s = lax.dot_general(q, kw, (((1,), (1,)), ((), ())),
                                preferred_element_type=jnp.float32) * (dh ** -0.5)
            s = jnp.where(mask, s + bias_ref[h], NEG_BIG)
            sink = sink_ref[h]
            m = jnp.maximum(jnp.max(s, axis=-1, keepdims=True), sink)
            p = jnp.exp(s - m)
            denom = jnp.sum(p, axis=-1, keepdims=True) + jnp.exp(sink - m)
            p = p / denom
            outs.append(jnp.dot(p.astype(bf), vw, preferred_element_type=jnp.float32))
    o = jnp.concatenate(outs, axis=1)
    o_ref[...] = (o * _silu(g_ref[...])).astype(o_ref.dtype)


def _attn(u, bias, sinks_l, bsz, t):
    w = WINDOW
    nb = t // w
    q_blk = _OUR_OFF["b_q"] // ATTN_WIDTH
    g_blk = _OUR_OFF["b_g"] // ATTN_WIDTH
    kv_blk = _OUR_OFF["b_k"] // (2 * KV_WIDTH)
    assert _OUR_OFF["b_v"] == _OUR_OFF["b_k"] + KV_WIDTH
    return pl.pallas_call(
        _attn_kernel,
        out_shape=jax.ShapeDtypeStruct((bsz * t, ATTN_WIDTH), jnp.bfloat16),
        grid=(bsz, nb),
        in_specs=[pl.BlockSpec(memory_space=pltpu.SMEM),
                  pl.BlockSpec((w, ATTN_WIDTH), lambda b, n: (b * nb + n, q_blk)),
                  pl.BlockSpec((w, 2 * KV_WIDTH), lambda b, n: (b * nb + jnp.maximum(n - 1, 0), kv_blk)),
                  pl.BlockSpec((w, 2 * KV_WIDTH), lambda b, n: (b * nb + n, kv_blk)),
                  pl.BlockSpec((w, ATTN_WIDTH), lambda b, n: (b * nb + n, g_blk)),
                  pl.BlockSpec((ATTN_HEADS, w, 2 * w), lambda b, n: (0, 0, 0))],
        out_specs=pl.BlockSpec((w, ATTN_WIDTH), lambda b, n: (b * nb + n, 0)),
        compiler_params=pltpu.CompilerParams(
            dimension_semantics=("parallel", "arbitrary"), vmem_limit_bytes=VMEM_LIMIT_BYTES),
        name="attn",
    )(sinks_l.astype(jnp.float32), u, u, u, u, bias)


def _tail_kernel(za_ref, zb_ref, cb_ref, cc_ref, cx_ref, cg_ref, hc_ref, hx_ref,
                 ma_ref, mb_ref, mc_ref, x_ref, wa_ref, wb_ref, wc_ref, wo_ref,
                 cw_ref, lng_ref, lnb_ref, xo_ref, xb_ref, *, tiles_per_seq):
    bf = jnp.bfloat16
    tm = x_ref.shape[0]
    h = cc_ref[...] * cx_ref[...]
    seq_start = (pl.program_id(0) % tiles_per_seq) == 0
    halo = jnp.where(seq_start, 0.0, hc_ref[...] * hx_ref[...])
    row = lax.broadcasted_iota(jnp.int32, (tm, CONV_WIDTH), 0)
    h1 = jnp.where(row >= 1, pltpu.roll(h, 1, 0), halo[7:8, :])
    h2 = jnp.where(row >= 2, pltpu.roll(h, 2, 0),
                   jnp.where(row == 1, halo[7:8, :], halo[6:7, :]))
    cw = cw_ref[...]
    y = cw[0:1, :] * h2 + cw[1:2, :] * h1 + cw[2:3, :] * h
    zc = (cb_ref[...] * y * _silu(cg_ref[...])).astype(bf)
    ya = jnp.dot(za_ref[...], wa_ref[...], preferred_element_type=jnp.float32)
    yb = jnp.dot(zb_ref[...], wb_ref[...], preferred_element_type=jnp.float32)
    yc = jnp.dot(zc, wc_ref[...], preferred_element_type=jnp.float32)
    merged = _sigmoid(ma_ref[...]) * ya + _sigmoid(mb_ref[...]) * yb + _sigmoid(mc_ref[...]) * yc
    yo = jnp.dot(merged.astype(bf), wo_ref[...], preferred_element_type=jnp.float32)
    r = ALPHA * x_ref[...] + yo
    mu = jnp.mean(r, axis=-1, keepdims=True)
    rc = r - mu
    var = jnp.mean(rc * rc, axis=-1, keepdims=True)
    xn = rc * lax.rsqrt(var + LN_EPS) * lng_ref[...] + lnb_ref[...]
    xo_ref[...] = xn
    xb_ref[...] = xn.astype(bf)


def _tail(za, zb, u, x, wa, wb, wc, wo, conv_w_l, ln_g_l, ln_b_l, t, layer):
    m = x.shape[0]
    tm = min(256, t)
    assert t % tm == 0
    tiles_per_seq = t // tm
    cw = CONV_WIDTH

    def seg(name, width):
        blk = _OUR_OFF[name] // width
        return pl.BlockSpec((tm, width), lambda i: (i, blk))

    def halo(name):
        blk = _OUR_OFF[name] // cw
        return pl.BlockSpec((8, cw), lambda i: (jnp.maximum(i * (tm // 8) - 1, 0), blk))

    def const(shape):
        return pl.BlockSpec(shape, lambda i: (0,) * len(shape), pipeline_mode=pl.Buffered(1))

    def weight(rows):
        return pl.BlockSpec((None, rows, D_MODEL), lambda i: (layer, 0, 0),
                            pipeline_mode=pl.Buffered(1))

    return pl.pallas_call(
        functools.partial(_tail_kernel, tiles_per_seq=tiles_per_seq),
        out_shape=(jax.ShapeDtypeStruct((m, D_MODEL), jnp.float32),
                   jax.ShapeDtypeStruct((m, D_MODEL), jnp.bfloat16)),
        grid=(m // tm,),
        in_specs=[pl.BlockSpec((tm, HGRN_WIDTH), lambda i: (i, 0)),
                  pl.BlockSpec((tm, ATTN_WIDTH), lambda i: (i, 0)),
                  seg("c_b", cw), seg("c_c", cw), seg("c_x", cw), seg("c_g", cw),
                  halo("c_c"), halo("c_x"),
                  seg("m_a", D_MODEL), seg("m_b", D_MODEL), seg("m_c", D_MODEL),
                  pl.BlockSpec((tm, D_MODEL), lambda i: (i, 0)),
                  weight(HGRN_WIDTH), weight(ATTN_WIDTH), weight(CONV_WIDTH), weight(D_MODEL),
                  const((CONV_K, cw)), const((1, D_MODEL)), const((1, D_MODEL))],
        out_specs=(pl.BlockSpec((tm, D_MODEL), lambda i: (i, 0)),
                   pl.BlockSpec((tm, D_MODEL), lambda i: (i, 0))),
        compiler_params=pltpu.CompilerParams(
            dimension_semantics=("parallel",), vmem_limit_bytes=VMEM_LIMIT_BYTES),
        name="tail",
    )(za, zb, u, u, u, u, u, u, u, u, u, x, wa, wb, wc, wo,
      conv_w_l, ln_g_l.reshape(1, D_MODEL), ln_b_l.reshape(1, D_MODEL))


def _permute_cast_w_in(w_in):
    parts = []
    for name in _OUR_ORDER:
        off, w = _REF_OFF[name]
        parts.append(w_in[:, :, off:off + w])
    return jnp.concatenate(parts, axis=-1).astype(jnp.bfloat16)


def kernel(x, w_in, w_proj_hgrn, w_proj_attn, w_proj_conv, w_out, lb_param, hgrn_norm_g,
           attn_sinks, conv_w, rel_bias, ln_g, ln_b):
    bsz, t, d = x.shape
    assert d == D_MODEL and t % WINDOW == 0
    bf = jnp.bfloat16
    lower, bias = _prep(lb_param, rel_bias)
    w_in_b = _permute_cast_w_in(w_in)
    wa_b, wb_b, wc_b, wo_b = (w.astype(bf) for w in (w_proj_hgrn, w_proj_attn, w_proj_conv, w_out))
    xf = x.reshape(bsz * t, d).astype(jnp.float32)
    xb = xf.astype(bf)
    for l in range(DEPTH):
        u = _inproj(xb, w_in_b, l)
        za = _hgrn(u, lower[l], hgrn_norm_g[l].astype(jnp.float32), bsz, t)
        zb = _attn(u, bias, attn_sinks[l], bsz, t)
        xf, xb = _tail(za, zb, u, xf, wa_b, wb_b, wc_b, wo_b,
                       conv_w[l].astype(jnp.float32), ln_g[l].astype(jnp.float32),
                       ln_b[l].astype(jnp.float32), t, l)
    return xf.reshape(bsz, t, d).astype(x.dtype)
```

```python
import functools
import math

import numpy as np
import jax
import jax.numpy as jnp
from jax import lax
from jax.experimental import pallas as pl
from jax.experimental.pallas import tpu as pltpu

D_MODEL = 2048
DEPTH = 4
HGRN_WIDTH = 1024
HGRN_HEAD_DIM = 128
HGRN_HEADS = HGRN_WIDTH // HGRN_HEAD_DIM
ATTN_HEAD_DIM = 64
ATTN_HEADS = 16
ATTN_KV_HEADS = 4
ATTN_GROUP = ATTN_HEADS // ATTN_KV_HEADS
ATTN_WIDTH = ATTN_HEADS * ATTN_HEAD_DIM
KV_WIDTH = ATTN_KV_HEADS * ATTN_HEAD_DIM
WINDOW = 128
CONV_WIDTH = 1024
CONV_K = 3
N_BUCKETS = 32
MAX_DISTANCE = 128
ALPHA = (2.0 * DEPTH) ** 0.25
LN_EPS = 1e-5
RMS_EPS = 1e-6
NEG_BIG = -1e30

_REF_SEGMENTS = (
    ("a_q", 1024), ("a_f", 1024), ("a_i", 1024), ("a_g", 1024),
    ("b_q", 1024), ("b_k", 256), ("b_v", 256), ("b_g", 1024),
    ("c_b", 1024), ("c_c", 1024), ("c_x", 1024), ("c_g", 1024),
    ("m_a", 2048), ("m_b", 2048), ("m_c", 2048),
)
_OUR_ORDER = ("a_q", "a_f", "a_i", "a_g", "b_q", "b_g", "c_b", "c_c", "c_x", "c_g",
              "m_a", "m_b", "m_c", "b_k", "b_v")
N_IN = sum(w for _, w in _REF_SEGMENTS)


def _segment_offsets():
    ref_off, off = {}, 0
    for name, w in _REF_SEGMENTS:
        ref_off[name] = (off, w)
        off += w
    our_off, off = {}, 0
    for name in _OUR_ORDER:
        w = ref_off[name][1]
        assert off % w == 0
        our_off[name] = off
        off += w
    return ref_off, our_off


_REF_OFF, _OUR_OFF = _segment_offsets()

LANES = 128
VMEM_LIMIT_BYTES = 56 * 1024 * 1024

HGRN_CHUNK = 64
HGRN_HEADS_PER_STEP = 4
HGRN_CHUNK_UNROLL = 4


def _t5_bucket_table():
    i = np.arange(WINDOW)[:, None]
    j = np.arange(2 * WINDOW)[None, :]
    rel = np.clip(WINDOW + i - j, 0, WINDOW - 1)
    max_exact = N_BUCKETS // 2
    logd = (np.log(np.maximum(rel, 1).astype(np.float32) / np.float32(max_exact))
            / np.float32(math.log(MAX_DISTANCE / max_exact))).astype(np.float32)
    large = max_exact + (logd * np.float32(N_BUCKETS - max_exact)).astype(np.int32)
    large = np.minimum(large, N_BUCKETS - 1)
    return np.where(rel < max_exact, rel, large).astype(np.int32)


def _sigmoid(x):
    return 1.0 / (1.0 + jnp.exp(-x))


def _silu(x):
    return x * _sigmoid(x)


def _prep_kernel(lb_ref, relb_ref, bucket_ref, lower_ref, bias_ref):
    lb = lb_ref[...]
    m = jnp.max(lb, axis=0, keepdims=True)
    e = jnp.exp(lb - m)
    soft = e / jnp.sum(e, axis=0, keepdims=True)
    acc = jnp.zeros_like(soft[0:1])
    for l in range(DEPTH):
        acc = acc + soft[l:l + 1]
        lower_ref[l:l + 1, :] = acc - soft[0:1]
    bucket = bucket_ref[...]
    for h in range(ATTN_HEADS):
        acc_b = jnp.zeros((WINDOW, 2 * WINDOW), jnp.float32)
        for bk in range(N_BUCKETS):
            acc_b = jnp.where(bucket == bk, relb_ref[bk, h], acc_b)
        bias_ref[h] = acc_b


def _prep(lb_param, rel_bias):
    bucket = jnp.asarray(_t5_bucket_table())
    return pl.pallas_call(
        _prep_kernel,
        out_shape=(jax.ShapeDtypeStruct((DEPTH, HGRN_WIDTH), jnp.float32),
                   jax.ShapeDtypeStruct((ATTN_HEADS, WINDOW, 2 * WINDOW), jnp.float32)),
        in_specs=[pl.BlockSpec(memory_space=pltpu.VMEM),
                  pl.BlockSpec(memory_space=pltpu.SMEM),
                  pl.BlockSpec(memory_space=pltpu.VMEM)],
        out_specs=(pl.BlockSpec(memory_space=pltpu.VMEM),
                   pl.BlockSpec(memory_space=pltpu.VMEM)),
        name="prep",
    )(lb_param.astype(jnp.float32), rel_bias.astype(jnp.float32), bucket)


def _inproj_kernel(x_ref, w_ref, o_ref):
    o_ref[...] = jnp.dot(x_ref[...], w_ref[...], preferred_element_type=jnp.float32)


def _inproj(x_bf16, w_bf16, layer):
    m = x_bf16.shape[0]
    tm = min(1024, m)
    tn = 1536
    assert m % tm == 0 and N_IN % tn == 0
    return pl.pallas_call(
        _inproj_kernel,
        out_shape=jax.ShapeDtypeStruct((m, N_IN), jnp.float32),
        grid=(m // tm, N_IN // tn),
        in_specs=[pl.BlockSpec((tm, D_MODEL), lambda i, j: (i, 0)),
                  pl.BlockSpec((None, D_MODEL, tn), lambda i, j: (layer, 0, j))],
        out_specs=pl.BlockSpec((tm, tn), lambda i, j: (i, j)),
        compiler_params=pltpu.CompilerParams(
            dimension_semantics=("parallel", "arbitrary"), vmem_limit_bytes=VMEM_LIMIT_BYTES),
        name="inproj",
    )(x_bf16, w_bf16)


_HGRN_LEVEL_SIZES = tuple(HGRN_CHUNK >> (i + 1) for i in range(HGRN_CHUNK.bit_length() - 1))
_HGRN_DIAG_LEVEL = len(_HGRN_LEVEL_SIZES)
_HGRN_N_SPANS = 2 + len(_HGRN_LEVEL_SIZES)


def _hgrn_static_tables():
    c = HGRN_CHUNK
    j = np.arange(c)[None, :]
    t = np.arange(c)[:, None]
    blocks = [(j <= t), (j > t)]
    level = np.full((c, c), -1, np.int32)
    level[np.arange(c), np.arange(c)] = _HGRN_DIAG_LEVEL
    for li, s in enumerate(_HGRN_LEVEL_SIZES):
        r = (t // (2 * s)) * 2 * s + s - 1
        later = ((t // s) % 2) == 1
        blocks.append(np.where(later, (j > r) & (j <= t), (j > t) & (j <= r)))
        tt, ss = np.arange(c)[:, None], np.arange(c)[None, :]
        sel = (tt // (2 * s) == ss // (2 * s)) & ((tt // s) % 2 == 1) & ((ss // s) % 2 == 0)
        level[sel] = li
    return np.concatenate(blocks, axis=0).astype(np.float32), level


def _hgrn_chunk(qk, v, e_blocks, state_t, level, odd_rows):
    c = HGRN_CHUNK
    bf = jnp.bfloat16
    nt_dims = (((1,), (1,)), ((), ()))
    qf, k = qk
    eb = e_blocks[0]
    e_end = e_blocks[1]
    qe = (qf * eb).astype(bf)
    o_inter = lax.dot_general(qe, state_t.astype(bf), nt_dims, preferred_element_type=jnp.float32)
    scores = lax.dot_general(qf.astype(bf), k.astype(bf), nt_dims, preferred_element_type=jnp.float32)
    scores = jnp.where(level == _HGRN_DIAG_LEVEL, scores, 0.0)
    for li, s in enumerate(_HGRN_LEVEL_SIZES):
        e = e_blocks[2 + li]
        if s >= 8:
            qk_sel = jnp.concatenate(
                [(qf if (j % 2) else k)[j * s:(j + 1) * s] for j in range(c // s)], axis=0)
        else:
            qk_sel = jnp.where(odd_rows[li], qf, k)
        m = (qk_sel * e).astype(bf)
        p = lax.dot_general(m, m, nt_dims, preferred_element_type=jnp.float32)
        scores = jnp.where(level == li, p, scores)
    o_intra = jnp.dot(scores.astype(bf), v.astype(bf), preferred_element_type=jnp.float32)
    kte = (k * e_end).astype(bf)
    upd = lax.dot_general(v.astype(bf), kte, (((0,), (0,)), ((), ())),
                          preferred_element_type=jnp.float32)
    new_state = state_t * eb[c - 1:c, :] + upd
    return o_inter + o_intra, new_state


def _hgrn_kernel(q_ref, f_ref, i_ref, g_ref, lb_ref, ng_ref, w_ref, lv_ref, o_ref, state_ref,
                 *, n_chunks, n_heads):
    c, d = HGRN_CHUNK, HGRN_HEAD_DIM
    bf = jnp.bfloat16

    @pl.when(pl.program_id(2) == 0)
    def _():
        state_ref[...] = jnp.zeros_like(state_ref)

    lb = lb_ref[0]
    ng = ng_ref[0]
    row = lax.broadcasted_iota(jnp.int32, (c, d), 0)
    odd_rows = [((row // s) % 2) == 1 for s in _HGRN_LEVEL_SIZES]

    def body(ci, carry):
        rows = pl.ds(pl.multiple_of(ci * c, c), c)
        f = lb + (1.0 - lb) * _sigmoid(f_ref[rows, :])
        g = jnp.log(f)
        g_hi = g.astype(bf)
        g_lo = (g - g_hi.astype(jnp.float32)).astype(bf)
        w = w_ref[...]
        expo = (jnp.dot(w, g_hi, preferred_element_type=jnp.float32)
                + jnp.dot(w, g_lo, preferred_element_type=jnp.float32))
        level = lv_ref[...]
        for hh in range(n_heads):
            hs = slice(hh * d, (hh + 1) * d)
            qf = _silu(q_ref[rows, hs]) * (d ** -0.5)
            k = 1.0 - f[:, hs]
            e_blocks = [jnp.exp(expo[j * c:(j + 1) * c, hs]) for j in range(_HGRN_N_SPANS)]
            o, new_state = _hgrn_chunk((qf, k), i_ref[rows, hs], e_blocks, state_ref[hh],
                                       level, odd_rows)
            state_ref[hh] = new_state
            o = o * lax.rsqrt(jnp.mean(o * o, axis=-1, keepdims=True) + RMS_EPS) * ng[:, hs]
            o_ref[rows, hs] = (o * _silu(g_ref[rows, hs])).astype(o_ref.dtype)
        return carry

    lax.fori_loop(0, n_chunks, body, 0, unroll=HGRN_CHUNK_UNROLL)


def _hgrn(u, lower_l, norm_g_l, bsz, t):
    tc = min(512, t)
    assert t % tc == 0 and tc % HGRN_CHUNK == 0
    nt = t // tc
    hb = HGRN_HEADS_PER_STEP
    wd = hb * HGRN_HEAD_DIM
    span_w, level = _hgrn_static_tables()

    def col_spec(name):
        base = _OUR_OFF[name] // wd
        return pl.BlockSpec((tc, wd), lambda b, h, i: (b * nt + i, base + h))

    vec_spec = pl.BlockSpec((1, 1, wd), lambda b, h, i: (h, 0, 0))
    return pl.pallas_call(
        functools.partial(_hgrn_kernel, n_chunks=tc // HGRN_CHUNK, n_heads=hb),
        out_shape=jax.ShapeDtypeStruct((bsz * t, HGRN_WIDTH), jnp.bfloat16),
        grid=(bsz, HGRN_HEADS // hb, nt),
        in_specs=[col_spec("a_q"), col_spec("a_f"), col_spec("a_i"), col_spec("a_g"),
                  vec_spec, vec_spec,
                  pl.BlockSpec(span_w.shape, lambda b, h, i: (0, 0)),
                  pl.BlockSpec(level.shape, lambda b, h, i: (0, 0))],
        out_specs=pl.BlockSpec((tc, wd), lambda b, h, i: (b * nt + i, h)),
        scratch_shapes=[pltpu.VMEM((hb, HGRN_HEAD_DIM, HGRN_HEAD_DIM), jnp.float32)],
        compiler_params=pltpu.CompilerParams(
            dimension_semantics=("parallel", "parallel", "arbitrary"),
            vmem_limit_bytes=VMEM_LIMIT_BYTES),
        name="hgrn",
    )(u, u, u, u, lower_l.reshape(HGRN_HEADS // hb, 1, wd), norm_g_l.reshape(HGRN_HEADS // hb, 1, wd),
      jnp.asarray(span_w, dtype=jnp.bfloat16), jnp.asarray(level))


def _attn_kernel(sink_ref, q_ref, kvp_ref, kvc_ref, g_ref, bias_ref, o_ref):
    w, dh = WINDOW, ATTN_HEAD_DIM
    bf = jnp.bfloat16
    first = pl.program_id(1) == 0
    i = lax.broadcasted_iota(jnp.int32, (w, 2 * w), 0)
    j = lax.broadcasted_iota(jnp.int32, (w, 2 * w), 1)
    rel = w + i - j
    mask = (rel >= 0) & (rel < w) & ((j >= w) | jnp.logical_not(first))
    kv_prev = kvp_ref[...]
    kv_cur = kvc_ref[...]
    outs = []
    for kvh in range(ATTN_KV_HEADS):
        ks = slice(kvh * dh, (kvh + 1) * dh)
        vs = slice(KV_WIDTH + kvh * dh, KV_WIDTH + (kvh + 1) * dh)
        kw = jnp.concatenate([kv_prev[:, ks], kv_cur[:, ks]], axis=0).astype(bf)
        vw = jnp.concatenate([kv_prev[:, vs], kv_cur[:, vs]], axis=0).astype(bf)
        for gi in range(ATTN_GROUP):
            h = kvh * ATTN_GROUP + gi
            q = q_ref[:, h * dh:(h + 1) * dh].astype(bf)
            s = lax.dot_general(q, kw, (((1,), (1,)), ((), ())),
                                preferred_element_type=jnp.float32) * (dh ** -0.5)
            s = jnp.where(mask, s + bias_ref[h], NEG_BIG)
            sink = sink_ref[h]
            m = jnp.maximum(jnp.max(s, axis=-1, keepdims=True), sink)
            p = jnp.exp(s - m)
            denom = jnp.sum(p, axis=-1, keepdims=True) + jnp.exp(sink - m)
            p = p / denom
            outs.append(jnp.dot(p.astype(bf), vw, preferred_element_type=jnp.float32))
    o = jnp.concatenate(outs, axis=1)
    o_ref[...] = (o * _silu(g_ref[...])).astype(o_ref.dtype)


def _attn(u, bias, sinks_l, bsz, t):
    w = WINDOW
    nb = t // w
    q_blk = _OUR_OFF["b_q"] // ATTN_WIDTH
    g_blk = _OUR_OFF["b_g"] // ATTN_WIDTH
    kv_blk = _OUR_OFF["b_k"] // (2 * KV_WIDTH)
    assert _OUR_OFF["b_v"] == _OUR_OFF["b_k"] + KV_WIDTH
    return pl.pallas_call(
        _attn_kernel,
        out_shape=jax.ShapeDtypeStruct((bsz * t, ATTN_WIDTH), jnp.bfloat16),
        grid=(bsz, nb),
        in_specs=[pl.BlockSpec(memory_space=pltpu.SMEM),
                  pl.BlockSpec((w, ATTN_WIDTH), lambda b, n: (b * nb + n, q_blk)),
                  pl.BlockSpec((w, 2 * KV_WIDTH), lambda b, n: (b * nb + jnp.maximum(n - 1, 0), kv_blk)),
                  pl.BlockSpec((w, 2 * KV_WIDTH), lambda b, n: (b * nb + n, kv_blk)),
                  pl.BlockSpec((w, ATTN_WIDTH), lambda b, n: (b * nb + n, g_blk)),
                  pl.BlockSpec((ATTN_HEADS, w, 2 * w), lambda b, n: (0, 0, 0))],
        out_specs=pl.BlockSpec((w, ATTN_WIDTH), lambda b, n: (b * nb + n, 0)),
        compiler_params=pltpu.CompilerParams(
            dimension_semantics=("parallel", "arbitrary"), vmem_limit_bytes=VMEM_LIMIT_BYTES),
        name="attn",
    )(sinks_l.astype(jnp.float32), u, u, u, u, bias)


def _tail_kernel(za_ref, zb_ref, cb_ref, cc_ref, cx_ref, cg_ref, hc_ref, hx_ref,
                 ma_ref, mb_ref, mc_ref, x_ref, wa_ref, wb_ref, wc_ref, wo_ref,
                 cw_ref, lng_ref, lnb_ref, xo_ref, xb_ref, *, tiles_per_seq):
    bf = jnp.bfloat16
    tm = x_ref.shape[0]
    h = cc_ref[...] * cx_ref[...]
    seq_start = (pl.program_id(0) % tiles_per_seq) == 0
    halo = jnp.where(seq_start, 0.0, hc_ref[...] * hx_ref[...])
    row = lax.broadcasted_iota(jnp.int32, (tm, CONV_WIDTH), 0)
    h1 = jnp.where(row >= 1, pltpu.roll(h, 1, 0), halo[7:8, :])
    h2 = jnp.where(row >= 2, pltpu.roll(h, 2, 0),
                   jnp.where(row == 1, halo[7:8, :], halo[6:7, :]))
    cw = cw_ref[...]
    y = cw[0:1, :] * h2 + cw[1:2, :] * h1 + cw[2:3, :] * h
    zc = (cb_ref[...] * y * _silu(cg_ref[...])).astype(bf)
    ya = jnp.dot(za_ref[...], wa_ref[...], preferred_element_type=jnp.float32)
    yb = jnp.dot(zb_ref[...], wb_ref[...], preferred_element_type=jnp.float32)
    yc = jnp.dot(zc, wc_ref[...], preferred_element_type=jnp.float32)
    merged = _sigmoid(ma_ref[...]) * ya + _sigmoid(mb_ref[...]) * yb + _sigmoid(mc_ref[...]) * yc
    yo = jnp.dot(merged.astype(bf), wo_ref[...], preferred_element_type=jnp.float32)
    r = ALPHA * x_ref[...] + yo
    mu = jnp.mean(r, axis=-1, keepdims=True)
    rc = r - mu
    var = jnp.mean(rc * rc, axis=-1, keepdims=True)
    xn = rc * lax.rsqrt(var + LN_EPS) * lng_ref[...] + lnb_ref[...]
    xo_ref[...] = xn
    xb_ref[...] = xn.astype(bf)


def _tail(za, zb, u, x, wa, wb, wc, wo, conv_w_l, ln_g_l, ln_b_l, t, layer):
    m = x.shape[0]
    tm = min(256, t)
    assert t % tm == 0
    tiles_per_seq = t // tm
    cw = CONV_WIDTH

    def seg(name, width):
        blk = _OUR_OFF[name] // width
        return pl.BlockSpec((tm, width), lambda i: (i, blk))

    def halo(name):
        blk = _OUR_OFF[name] // cw
        return pl.BlockSpec((8, cw), lambda i: (jnp.maximum(i * (tm // 8) - 1, 0), blk))

    def const(shape):
        return pl.BlockSpec(shape, lambda i: (0,) * len(shape), pipeline_mode=pl.Buffered(1))

    def weight(rows):
        return pl.BlockSpec((None, rows, D_MODEL), lambda i: (layer, 0, 0),
                            pipeline_mode=pl.Buffered(1))

    return pl.pallas_call(
        functools.partial(_tail_kernel, tiles_per_seq=tiles_per_seq),
        out_shape=(jax.ShapeDtypeStruct((m, D_MODEL), jnp.float32),
                   jax.ShapeDtypeStruct((m, D_MODEL), jnp.bfloat16)),
        grid=(m // tm,),
        in_specs=[pl.BlockSpec((tm, HGRN_WIDTH), lambda i: (i, 0)),
                  pl.BlockSpec((tm, ATTN_WIDTH), lambda i: (i, 0)),
                  seg("c_b", cw), seg("c_c", cw), seg("c_x", cw), seg("c_g", cw),
                  halo("c_c"), halo("c_x"),
                  seg("m_a", D_MODEL), seg("m_b", D_MODEL), seg("m_c", D_MODEL),
                  pl.BlockSpec((tm, D_MODEL), lambda i: (i, 0)),
                  weight(HGRN_WIDTH), weight(ATTN_WIDTH), weight(CONV_WIDTH), weight(D_MODEL),
                  const((CONV_K, cw)), const((1, D_MODEL)), const((1, D_MODEL))],
        out_specs=(pl.BlockSpec((tm, D_MODEL), lambda i: (i, 0)),
                   pl.BlockSpec((tm, D_MODEL), lambda i: (i, 0))),
        compiler_params=pltpu.CompilerParams(
            dimension_semantics=("parallel",), vmem_limit_bytes=VMEM_LIMIT_BYTES),
        name="tail",
    )(za, zb, u, u, u, u, u, u, u, u, u, x, wa, wb, wc, wo,
      conv_w_l, ln_g_l.reshape(1, D_MODEL), ln_b_l.reshape(1, D_MODEL))


def _permute_cast_w_in(w_in):
    parts = []
    for name in _OUR_ORDER:
        off, w = _REF_OFF[name]
        parts.append(w_in[:, :, off:off + w])
    return jnp.concatenate(parts, axis=-1).astype(jnp.bfloat16)


def kernel(x, w_in, w_proj_hgrn, w_proj_attn, w_proj_conv, w_out, lb_param, hgrn_norm_g,
           attn_sinks, conv_w, rel_bias, ln_g, ln_b):
    bsz, t, d = x.shape
    assert d == D_MODEL and t % WINDOW == 0
    bf = jnp.bfloat16
    lower, bias = _prep(lb_param, rel_bias)
    w_in_b = _permute_cast_w_in(w_in)
    wa_b, wb_b, wc_b, wo_b = (w.astype(bf) for w in (w_proj_hgrn, w_proj_attn, w_proj_conv, w_out))
    xf = x.reshape(bsz * t, d).astype(jnp.float32)
    xb = xf.astype(bf)
    for l in range(DEPTH):
        u = _inproj(xb, w_in_b, l)
        za = _hgrn(u, lower[l], hgrn_norm_g[l].astype(jnp.float32), bsz, t)
        zb = _attn(u, bias, attn_sinks[l], bsz, t)
        xf, xb = _tail(za, zb, u, xf, wa_b, wb_b, wc_b, wo_b,
                       conv_w[l].astype(jnp.float32), ln_g[l].astype(jnp.float32),
                       ln_b[l].astype(jnp.float32), t, l)
    return xf.reshape(bsz, t, d).astype(x.dtype)
```

```python
import functools
import math

import numpy as np
import jax
import jax.numpy as jnp
from jax import lax
from jax.experimental import pallas as pl
from jax.experimental.pallas import tpu as pltpu

D_MODEL = 2048
DEPTH = 4
HGRN_WIDTH = 1024
HGRN_HEAD_DIM = 128
HGRN_HEADS = HGRN_WIDTH // HGRN_HEAD_DIM
ATTN_HEAD_DIM = 64
ATTN_HEADS = 16
ATTN_KV_HEADS = 4
ATTN_GROUP = ATTN_HEADS // ATTN_KV_HEADS
ATTN_WIDTH = ATTN_HEADS * ATTN_HEAD_DIM
KV_WIDTH = ATTN_KV_HEADS * ATTN_HEAD_DIM
WINDOW = 128
CONV_WIDTH = 1024
CONV_K = 3
N_BUCKETS = 32
MAX_DISTANCE = 128
ALPHA = (2.0 * DEPTH) ** 0.25
LN_EPS = 1e-5
RMS_EPS = 1e-6
NEG_BIG = -1e30

_REF_SEGMENTS = (
    ("a_q", 1024), ("a_f", 1024), ("a_i", 1024), ("a_g", 1024),
    ("b_q", 1024), ("b_k", 256), ("b_v", 256), ("b_g", 1024),
    ("c_b", 1024), ("c_c", 1024), ("c_x", 1024), ("c_g", 1024),
    ("m_a", 2048), ("m_b", 2048), ("m_c", 2048),
)
_OUR_ORDER = ("a_q", "a_f", "a_i", "a_g", "b_q", "b_g", "c_b", "c_c", "c_x", "c_g",
              "m_a", "m_b", "m_c", "b_k", "b_v")
N_IN = sum(w for _, w in _REF_SEGMENTS)


def _segment_offsets():
    ref_off, off = {}, 0
    for name, w in _REF_SEGMENTS:
        ref_off[name] = (off, w)
        off += w
    our_off, off = {}, 0
    for name in _OUR_ORDER:
        w = ref_off[name][1]
        assert off % w == 0
        our_off[name] = off
        off += w
    return ref_off, our_off


_REF_OFF, _OUR_OFF = _segment_offsets()

LANES = 128
VMEM_LIMIT_BYTES = 56 * 1024 * 1024

HGRN_CHUNK = 64
HGRN_HEADS_PER_STEP = 4
HGRN_CHUNK_UNROLL = 4


def _t5_bucket_table():
    i = np.arange(WINDOW)[:, None]
    j = np.arange(2 * WINDOW)[None, :]
    rel = np.clip(WINDOW + i - j, 0, WINDOW - 1)
    max_exact = N_BUCKETS // 2
    logd = (np.log(np.maximum(rel, 1).astype(np.float32) / np.float32(max_exact))
            / np.float32(math.log(MAX_DISTANCE / max_exact))).astype(np.float32)
    large = max_exact + (logd * np.float32(N_BUCKETS - max_exact)).astype(np.int32)
    large = np.minimum(large, N_BUCKETS - 1)
    return np.where(rel < max_exact, rel, large).astype(np.int32)


def _sigmoid(x):
    return 1.0 / (1.0 + jnp.exp(-x))


def _silu(x):
    return x * _sigmoid(x)


def _prep_kernel(lb_ref, relb_ref, bucket_ref, lower_ref, bias_ref):
    lb = lb_ref[...]
    m = jnp.max(lb, axis=0, keepdims=True)
    e = jnp.exp(lb - m)
    soft = e / jnp.sum(e, axis=0, keepdims=True)
    acc = jnp.zeros_like(soft[0:1])
    for l in range(DEPTH):
        acc = acc + soft[l:l + 1]
        lower_ref[l:l + 1, :] = acc - soft[0:1]
    bucket = bucket_ref[...]
    i = lax.broadcasted_iota(jnp.int32, (WINDOW, 2 * WINDOW), 0)
    j = lax.broadcasted_iota(jnp.int32, (WINDOW, 2 * WINDOW), 1)
    rel = WINDOW + i - j
    band = (rel >= 0) & (rel < WINDOW)
    for h in range(ATTN_HEADS):
        acc_b = jnp.zeros((WINDOW, 2 * WINDOW), jnp.float32)
        for bk in range(N_BUCKETS):
            acc_b = jnp.where(bucket == bk, relb_ref[bk, h], acc_b)
        bias_ref[0, h] = jnp.where(band, acc_b, NEG_BIG)
        bias_ref[1, h] = jnp.where(band & (j >= WINDOW), acc_b, NEG_BIG)


def _prep(lb_param, rel_bias):
    bucket = jnp.asarray(_t5_bucket_table())
    return pl.pallas_call(
        _prep_kernel,
        out_shape=(jax.ShapeDtypeStruct((DEPTH, HGRN_WIDTH), jnp.float32),
                   jax.ShapeDtypeStruct((2, ATTN_HEADS, WINDOW, 2 * WINDOW), jnp.float32)),
        in_specs=[pl.BlockSpec(memory_space=pltpu.VMEM),
                  pl.BlockSpec(memory_space=pltpu.SMEM),
                  pl.BlockSpec(memory_space=pltpu.VMEM)],
        out_specs=(pl.BlockSpec(memory_space=pltpu.VMEM),
                   pl.BlockSpec(memory_space=pltpu.VMEM)),
        name="prep",
    )(lb_param.astype(jnp.float32), rel_bias.astype(jnp.float32), bucket)


TAIL_MERGE_CHUNK = 2048
INPROJ_TN = 512


def _ref_tile_of_our_tile():
    table = []
    for name in _OUR_ORDER:
        off, w = _REF_OFF[name]
        if name == "b_v":
            continue
        w = w + _REF_OFF["b_v"][1] if name == "b_k" else w
        assert off % INPROJ_TN == 0 and w % INPROJ_TN == 0
        table.extend(range(off // INPROJ_TN, (off + w) // INPROJ_TN))
    assert sorted(table) == list(range(N_IN // INPROJ_TN))
    return table


def _inproj_kernel(x_ref, w_ref, o_ref):
    o_ref[...] = jnp.dot(x_ref[...], w_ref[...].astype(jnp.bfloat16),
                         preferred_element_type=jnp.float32)


def _inproj(x_bf16, w_in, layer):
    m = x_bf16.shape[0]
    tm = min(2048, m)
    tn = INPROJ_TN
    assert m % tm == 0 and N_IN % tn == 0
    table = _ref_tile_of_our_tile()
    kv_ref = _REF_OFF["b_k"][0] // tn
    n_tiles = N_IN // tn
    assert table == [j if j < kv_ref else j + 1 for j in range(n_tiles - 1)] + [kv_ref]

    def w_map(i, j):
        return (layer, 0, jnp.where(j < kv_ref, j, jnp.where(j < n_tiles - 1, j + 1, kv_ref)))

    return pl.pallas_call(
        _inproj_kernel,
        out_shape=jax.ShapeDtypeStruct((m, N_IN), jnp.float32),
        grid=(m // tm, n_tiles),
        in_specs=[pl.BlockSpec((tm, D_MODEL), lambda i, j: (i, 0)),
                  pl.BlockSpec((None, D_MODEL, tn), w_map)],
        out_specs=pl.BlockSpec((tm, tn), lambda i, j: (i, j)),
        compiler_params=pltpu.CompilerParams(
            dimension_semantics=("parallel", "arbitrary"), vmem_limit_bytes=VMEM_LIMIT_BYTES),
        name="inproj",
    )(x_bf16, w_in)


_HGRN_LEVEL_SIZES = tuple(HGRN_CHUNK >> (i + 1) for i in range(HGRN_CHUNK.bit_length() - 1))
_HGRN_DIAG_LEVEL = len(_HGRN_LEVEL_SIZES)
_HGRN_N_SPANS = 2 + len(_HGRN_LEVEL_SIZES) - 1


def _hgrn_static_tables():
    c = HGRN_CHUNK
    j = np.arange(c)[None, :]
    t = np.arange(c)[:, None]
    blocks = [(j <= t), (j > t)]
    level = np.full((c, c), -1, np.int32)
    level[np.arange(c), np.arange(c)] = _HGRN_DIAG_LEVEL
    for li, s in enumerate(_HGRN_LEVEL_SIZES):
        r = (t // (2 * s)) * 2 * s + s - 1
        later = ((t // s) % 2) == 1
        if s > 1:
            blocks.append(np.where(later, (j > r) & (j <= t), (j > t) & (j <= r)))
        tt, ss = np.arange(c)[:, None], np.arange(c)[None, :]
        sel = (tt // (2 * s) == ss // (2 * s)) & ((tt // s) % 2 == 1) & ((ss // s) % 2 == 0)
        level[sel] = li
    span = np.concatenate(blocks, axis=0).astype(np.float32)
    return np.concatenate([span, span], axis=1), level


def _hgrn_chunk(qk, v, e_blocks, state_t, level, odd_rows):
    c = HGRN_CHUNK
    bf = jnp.bfloat16
    nt_dims = (((1,), (1,)), ((), ()))
    qf, k, f = qk
    eb = e_blocks[0]
    e_end = e_blocks[1]
    qe = (qf * eb).astype(bf)
    o_inter = lax.dot_general(qe, state_t.astype(bf), nt_dims, preferred_element_type=jnp.float32)
    scores = lax.dot_general(qf.astype(bf), k.astype(bf), nt_dims, preferred_element_type=jnp.float32)
    scores = jnp.where(level == _HGRN_DIAG_LEVEL, scores, 0.0)
    for li, s in enumerate(_HGRN_LEVEL_SIZES):
        if s >= 8:
            qk_sel = jnp.concatenate(
                [(qf if (j % 2) else k)[j * s:(j + 1) * s] for j in range(c // s)], axis=0)
            m = (qk_sel * e_blocks[2 + li]).astype(bf)
        elif s > 1:
            m = (jnp.where(odd_rows[li], qf, k) * e_blocks[2 + li]).astype(bf)
        else:
            m = jnp.where(odd_rows[li], qf * f, k).astype(bf)
        p = lax.dot_general(m, m, nt_dims, preferred_element_type=jnp.float32)
        scores = jnp.where(level == li, p, scores)
    o_intra = jnp.dot(scores.astype(bf), v.astype(bf), preferred_element_type=jnp.float32)
    kte = (k * e_end).astype(bf)
    upd = lax.dot_general(v.astype(bf), kte, (((0,), (0,)), ((), ())),
                          preferred_element_type=jnp.float32)
    new_state = state_t * eb[c - 1:c, :] + upd
    return o_inter + o_intra, new_state


def _hgrn_kernel(q_ref, f_ref, i_ref, g_ref, lb_ref, ng_ref, w_ref, lv_ref, o_ref, state_ref,
                 *, n_chunks, n_heads):
    c, d = HGRN_CHUNK, HGRN_HEAD_DIM
    bf = jnp.bfloat16

    @pl.when(pl.program_id(2) == 0)
    def _():
        state_ref[...] = jnp.zeros_like(state_ref)

    lb = lb_ref[0]
    ng = ng_ref[0]
    row = lax.broadcasted_iota(jnp.int32, (c, d), 0)
    odd_rows = [((row // s) % 2) == 1 for s in _HGRN_LEVEL_SIZES]

    def body(ci, carry):
        rows = pl.ds(pl.multiple_of(ci * c, c), c)
        f = lb + (1.0 - lb) * _sigmoid(f_ref[rows, :])
        g = jnp.log(f)
        g_hi = g.astype(bf)
        g_lo = (g - g_hi.astype(jnp.float32)).astype(bf)
        expo = jnp.dot(w_ref[...], jnp.concatenate([g_hi, g_lo], axis=0),
                       preferred_element_type=jnp.float32)
        level = lv_ref[...]
        for hh in range(n_heads):
            hs = slice(hh * d, (hh + 1) * d)
            qf = _silu(q_ref[rows, hs]) * (d ** -0.5)
            k = 1.0 - f[:, hs]
            e_blocks = [jnp.exp(expo[j * c:(j + 1) * c, hs]) for j in range(_HGRN_N_SPANS)]
            o, new_state = _hgrn_chunk((qf, k, f[:, hs]), i_ref[rows, hs], e_blocks, state_ref[hh],
                                       level, odd_rows)
            state_ref[hh] = new_state
            o = o * lax.rsqrt(jnp.mean(o * o, axis=-1, keepdims=True) + RMS_EPS) * ng[:, hs]
            o_ref[rows, hs] = (o * _silu(g_ref[rows, hs])).astype(o_ref.dtype)
        return carry

    lax.fori_loop(0, n_chunks, body, 0, unroll=HGRN_CHUNK_UNROLL)


def _hgrn(u, lower_l, norm_g_l, bsz, t):
    tc = min(512, t)
    assert t % tc == 0 and tc % HGRN_CHUNK == 0
    nt = t // tc
    hb = HGRN_HEADS_PER_STEP
    wd = hb * HGRN_HEAD_DIM
    span_w, level = _hgrn_static_tables()

    def col_spec(name):
        base = _OUR_OFF[name] // wd
        return pl.BlockSpec((tc, wd), lambda b, h, i: (b * nt + i, base + h))

    vec_spec = pl.BlockSpec((1, 1, wd), lambda b, h, i: (h, 0, 0))
    return pl.pallas_call(
        functools.partial(_hgrn_kernel, n_chunks=tc // HGRN_CHUNK, n_heads=hb),
        out_shape=jax.ShapeDtypeStruct((bsz * t, HGRN_WIDTH), jnp.bfloat16),
        grid=(bsz, HGRN_HEADS // hb, nt),
        in_specs=[col_spec("a_q"), col_spec("a_f"), col_spec("a_i"), col_spec("a_g"),
                  vec_spec, vec_spec,
                  pl.BlockSpec(span_w.shape, lambda b, h, i: (0, 0)),
                  pl.BlockSpec(level.shape, lambda b, h, i: (0, 0))],
        out_specs=pl.BlockSpec((tc, wd), lambda b, h, i: (b * nt + i, h)),
        scratch_shapes=[pltpu.VMEM((hb, HGRN_HEAD_DIM, HGRN_HEAD_DIM), jnp.float32)],
        compiler_params=pltpu.CompilerParams(
            dimension_semantics=("parallel", "parallel", "arbitrary"),
            vmem_limit_bytes=VMEM_LIMIT_BYTES),
        name="hgrn",
    )(u, u, u, u, lower_l.reshape(HGRN_HEADS // hb, 1, wd), norm_g_l.reshape(HGRN_HEADS // hb, 1, wd),
      jnp.asarray(span_w, dtype=jnp.bfloat16), jnp.asarray(level))


def _attn_kernel(sink_ref, q_ref, kvp_ref, kvc_ref, g_ref, bias_ref, o_ref):
    dh = ATTN_HEAD_DIM
    bf = jnp.bfloat16
    kv_prev = kvp_ref[...]
    kv_cur = kvc_ref[...]
    outs = []
    for kvh in range(ATTN_KV_HEADS):
        ks = slice(kvh * dh, (kvh + 1) * dh)
        vs = slice(KV_WIDTH + kvh * dh, KV_WIDTH + (kvh + 1) * dh)
        kw = jnp.concatenate([kv_prev[:, ks], kv_cur[:, ks]], axis=0).astype(bf)
        vw = jnp.concatenate([kv_prev[:, vs], kv_cur[:, vs]], axis=0).astype(bf)
        for gi in range(ATTN_GROUP):
            h = kvh * ATTN_GROUP + gi
            q = (q_ref[:, h * dh:(h + 1) * dh] * (dh ** -0.5)).astype(bf)
            s = lax.dot_general(q, kw, (((1,), (1,)), ((), ())),
                                preferred_element_type=jnp.float32) + bias_ref[h]
            sink = sink_ref[h]
            m = jnp.maximum(jnp.max(s, axis=-1, keepdims=True), sink)
            p = jnp.exp(s - m)
            denom = jnp.sum(p, axis=-1, keepdims=True) + jnp.exp(sink - m)
            pv = jnp.dot(p.astype(bf), vw, preferred_element_type=jnp.float32)
            outs.append(pv * (1.0 / denom))
    o = jnp.concatenate(outs, axis=1)
    o_ref[...] = (o * _silu(g_ref[...])).astype(o_ref.dtype)


def _attn(u, bias, sinks_l, bsz, t):
    w = WINDOW
    nb = t // w
    q_blk = _OUR_OFF["b_q"] // ATTN_WIDTH
    g_blk = _OUR_OFF["b_g"] // ATTN_WIDTH
    kv_blk = _OUR_OFF["b_k"] // (2 * KV_WIDTH)
    assert _OUR_OFF["b_v"] == _OUR_OFF["b_k"] + KV_WIDTH
    return pl.pallas_call(
        _attn_kernel,
        out_shape=jax.ShapeDtypeStruct((bsz * t, ATTN_WIDTH), jnp.bfloat16),
        grid=(bsz, nb),
        in_specs=[pl.BlockSpec(memory_space=pltpu.SMEM),
                  pl.BlockSpec((w, ATTN_WIDTH), lambda b, n: (b * nb + n, q_blk)),
                  pl.BlockSpec((w, 2 * KV_WIDTH), lambda b, n: (b * nb + jnp.maximum(n - 1, 0), kv_blk)),
                  pl.BlockSpec((w, 2 * KV_WIDTH), lambda b, n: (b * nb + n, kv_blk)),
                  pl.BlockSpec((w, ATTN_WIDTH), lambda b, n: (b * nb + n, g_blk)),
                  pl.BlockSpec((None, ATTN_HEADS, w, 2 * w),
                               lambda b, n: (jnp.where(n == 0, 1, 0), 0, 0, 0))],
        out_specs=pl.BlockSpec((w, ATTN_WIDTH), lambda b, n: (b * nb + n, 0)),
        compiler_params=pltpu.CompilerParams(
            dimension_semantics=("parallel", "arbitrary"), vmem_limit_bytes=VMEM_LIMIT_BYTES),
        name="attn",
    )(sinks_l.astype(jnp.float32), u, u, u, u, bias)


def _tail_kernel(za_ref, zb_ref, cb_ref, cc_ref, cx_ref, cg_ref, hc_ref, hx_ref,
                 ma_ref, mb_ref, mc_ref, x_ref, wa_ref, wb_ref, wc_ref, wo_ref,
                 cw_ref, lng_ref, lnb_ref, xo_ref, xb_ref, *, tiles_per_seq):
    bf = jnp.bfloat16
    tm = x_ref.shape[0]
    h = cc_ref[...] * cx_ref[...]
    seq_start = (pl.program_id(0) % tiles_per_seq) == 0
    halo = jnp.where(seq_start, 0.0, hc_ref[...] * hx_ref[...])
    row = lax.broadcasted_iota(jnp.int32, (tm, CONV_WIDTH), 0)
    h1 = jnp.where(row >= 1, pltpu.roll(h, 1, 0), halo[7:8, :])
    h2 = jnp.where(row >= 2, pltpu.roll(h, 2, 0),
                   jnp.where(row == 1, halo[7:8, :], halo[6:7, :]))
    cw = cw_ref[...]
    y = cw[0:1, :] * h2 + cw[1:2, :] * h1 + cw[2:3, :] * h
    zc = (cb_ref[...] * y * _silu(cg_ref[...])).astype(bf)
    za, zb = za_ref[...], zb_ref[...]
    yo = None
    for c0 in range(0, D_MODEL, TAIL_MERGE_CHUNK):
        cs = slice(c0, c0 + TAIL_MERGE_CHUNK)
        ya = jnp.dot(za, wa_ref[:, cs], preferred_element_type=jnp.float32)
        yb = jnp.dot(zb, wb_ref[:, cs], preferred_element_type=jnp.float32)
        yc = jnp.dot(zc, wc_ref[:, cs], preferred_element_type=jnp.float32)
        merged = (_sigmoid(ma_ref[:, cs]) * ya + _sigmoid(mb_ref[:, cs]) * yb
                  + _sigmoid(mc_ref[:, cs]) * yc)
        part = jnp.dot(merged.astype(bf), wo_ref[cs, :], preferred_element_type=jnp.float32)
        yo = part if yo is None else yo + part
    r = ALPHA * x_ref[...] + yo
    mu = jnp.mean(r, axis=-1, keepdims=True)
    rc = r - mu
    var = jnp.mean(rc * rc, axis=-1, keepdims=True)
    xn = rc * lax.rsqrt(var + LN_EPS) * lng_ref[...] + lnb_ref[...]
    xo_ref[...] = xn
    xb_ref[...] = xn.astype(bf)


def _tail(za, zb, u, x, wa, wb, wc, wo, conv_w_l, ln_g_l, ln_b_l, t, layer):
    m = x.shape[0]
    tm = min(256, t)
    assert t % tm == 0
    tiles_per_seq = t // tm
    cw = CONV_WIDTH

    def seg(name, width):
        blk = _OUR_OFF[name] // width
        return pl.BlockSpec((tm, width), lambda i: (i, blk))

    def halo(name):
        blk = _OUR_OFF[name] // cw
        return pl.BlockSpec((8, cw), lambda i: (jnp.maximum(i * (tm // 8) - 1, 0), blk))

    def const(shape):
        return pl.BlockSpec(shape, lambda i: (0,) * len(shape), pipeline_mode=pl.Buffered(1))

    def weight(rows):
        return pl.BlockSpec((None, rows, D_MODEL), lambda i: (layer, 0, 0),
                            pipeline_mode=pl.Buffered(1))

    return pl.pallas_call(
        functools.partial(_tail_kernel, tiles_per_seq=tiles_per_seq),
        out_shape=(jax.ShapeDtypeStruct((m, D_MODEL), jnp.float32),
                   jax.ShapeDtypeStruct((m, D_MODEL), jnp.bfloat16)),
        grid=(m // tm,),
        in_specs=[pl.BlockSpec((tm, HGRN_WIDTH), lambda i: (i, 0)),
                  pl.BlockSpec((tm, ATTN_WIDTH), lambda i: (i, 0)),
                  seg("c_b", cw), seg("c_c", cw), seg("c_x", cw), seg("c_g", cw),
                  halo("c_c"), halo("c_x"),
                  seg("m_a", D_MODEL), seg("m_b", D_MODEL), seg("m_c", D_MODEL),
                  pl.BlockSpec((tm, D_MODEL), lambda i: (i, 0)),
                  weight(HGRN_WIDTH), weight(ATTN_WIDTH), weight(CONV_WIDTH), weight(D_MODEL),
                  const((CONV_K, cw)), const((1, D_MODEL)), const((1, D_MODEL))],
        out_specs=(pl.BlockSpec((tm, D_MODEL), lambda i: (i, 0)),
                   pl.BlockSpec((tm, D_MODEL), lambda i: (i, 0))),
        compiler_params=pltpu.CompilerParams(
            dimension_semantics=("parallel",), vmem_limit_bytes=VMEM_LIMIT_BYTES),
        name="tail",
    )(za, zb, u, u, u, u, u, u, u, u, u, x, wa, wb, wc, wo,
      conv_w_l, ln_g_l.reshape(1, D_MODEL), ln_b_l.reshape(1, D_MODEL))


def kernel(x, w_in, w_proj_hgrn, w_proj_attn, w_proj_conv, w_out, lb_param, hgrn_norm_g,
           attn_sinks, conv_w, rel_bias, ln_g, ln_b):
    bsz, t, d = x.shape
    assert d == D_MODEL and t % WINDOW == 0
    bf = jnp.bfloat16
    lower, bias = _prep(lb_param, rel_bias)
    w_in = w_in.astype(jnp.float32)
    wa_b, wb_b, wc_b, wo_b = (w.astype(bf) for w in (w_proj_hgrn, w_proj_attn, w_proj_conv, w_out))
    xf = x.reshape(bsz * t, d).astype(jnp.float32)
    xb = xf.astype(bf)
    for l in range(DEPTH):
        u = _inproj(xb, w_in, l)
        za = _hgrn(u, lower[l], hgrn_norm_g[l].astype(jnp.float32), bsz, t)
        zb = _attn(u, bias, attn_sinks[l], bsz, t)
        xf, xb = _tail(za, zb, u, xf, wa_b, wb_b, wc_b, wo_b,
                       conv_w[l].astype(jnp.float32), ln_g[l].astype(jnp.float32),
                       ln_b[l].astype(jnp.float32), t, l)
    return xf.reshape(bsz, t, d).astype(x.dtype)
```

```python
import functools
import math

import numpy as np
import jax
import jax.numpy as jnp
from jax import lax
from jax.experimental import pallas as pl
from jax.experimental.pallas import tpu as pltpu

D_MODEL = 2048
DEPTH = 4
HGRN_WIDTH = 1024
HGRN_HEAD_DIM = 128
HGRN_HEADS = HGRN_WIDTH // HGRN_HEAD_DIM
ATTN_HEAD_DIM = 64
ATTN_HEADS = 16
ATTN_KV_HEADS = 4
ATTN_GROUP = ATTN_HEADS // ATTN_KV_HEADS
ATTN_WIDTH = ATTN_HEADS * ATTN_HEAD_DIM
KV_WIDTH = ATTN_KV_HEADS * ATTN_HEAD_DIM
WINDOW = 128
CONV_WIDTH = 1024
CONV_K = 3
N_BUCKETS = 32
MAX_DISTANCE = 128
ALPHA = (2.0 * DEPTH) ** 0.25
LN_EPS = 1e-5
RMS_EPS = 1e-6
NEG_BIG = -1e30

_REF_SEGMENTS = (
    ("a_q", 1024), ("a_f", 1024), ("a_i", 1024), ("a_g", 1024),
    ("b_q", 1024), ("b_k", 256), ("b_v", 256), ("b_g", 1024),
    ("c_b", 1024), ("c_c", 1024), ("c_x", 1024), ("c_g", 1024),
    ("m_a", 2048), ("m_b", 2048), ("m_c", 2048),
)
_OUR_ORDER = ("a_q", "a_f", "a_i", "a_g", "b_q", "b_g", "c_b", "c_c", "c_x", "c_g",
              "m_a", "m_b", "m_c", "b_k", "b_v")
N_IN = sum(w for _, w in _REF_SEGMENTS)


def _segment_offsets():
    ref_off, off = {}, 0
    for name, w in _REF_SEGMENTS:
        ref_off[name] = (off, w)
        off += w
    our_off, off = {}, 0
    for name in _OUR_ORDER:
        w = ref_off[name][1]
        assert off % w == 0
        our_off[name] = off
        off += w
    return ref_off, our_off


_REF_OFF, _OUR_OFF = _segment_offsets()

LANES = 128
VMEM_LIMIT_BYTES = 56 * 1024 * 1024

HGRN_CHUNK = 64
HGRN_HEADS_PER_STEP = 4
HGRN_CHUNK_UNROLL = 4


def _t5_bucket_table():
    i = np.arange(WINDOW)[:, None]
    j = np.arange(2 * WINDOW)[None, :]
    rel = np.clip(WINDOW + i - j, 0, WINDOW - 1)
    max_exact = N_BUCKETS // 2
    logd = (np.log(np.maximum(rel, 1).astype(np.float32) / np.float32(max_exact))
            / np.float32(math.log(MAX_DISTANCE / max_exact))).astype(np.float32)
    large = max_exact + (logd * np.float32(N_BUCKETS - max_exact)).astype(np.int32)
    large = np.minimum(large, N_BUCKETS - 1)
    return np.where(rel < max_exact, rel, large).astype(np.int32)


def _sigmoid(x):
    return 1.0 / (1.0 + jnp.exp(-x))


def _silu(x):
    return x * _sigmoid(x)


def _prep_kernel(lb_ref, relb_ref, bucket_ref, lower_ref, bias_ref):
    lb = lb_ref[...]
    m = jnp.max(lb, axis=0, keepdims=True)
    e = jnp.exp(lb - m)
    soft = e / jnp.sum(e, axis=0, keepdims=True)
    acc = jnp.zeros_like(soft[0:1])
    for l in range(DEPTH):
        acc = acc + soft[l:l + 1]
        lower_ref[l:l + 1, :] = acc - soft[0:1]
    bucket = bucket_ref[...]
    i = lax.broadcasted_iota(jnp.int32, (WINDOW, 2 * WINDOW), 0)
    j = lax.broadcasted_iota(jnp.int32, (WINDOW, 2 * WINDOW), 1)
    rel = WINDOW + i - j
    band = (rel >= 0) & (rel < WINDOW)
    for h in range(ATTN_HEADS):
        acc_b = jnp.zeros((WINDOW, 2 * WINDOW), jnp.float32)
        for bk in range(N_BUCKETS):
            acc_b = jnp.where(bucket == bk, relb_ref[bk, h], acc_b)
        bias_ref[0, h] = jnp.where(band, acc_b, NEG_BIG)
        bias_ref[1, h] = jnp.where(band & (j >= WINDOW), acc_b, NEG_BIG)


def _prep(lb_param, rel_bias):
    bucket = jnp.asarray(_t5_bucket_table())
    return pl.pallas_call(
        _prep_kernel,
        out_shape=(jax.ShapeDtypeStruct((DEPTH, HGRN_WIDTH), jnp.float32),
                   jax.ShapeDtypeStruct((2, ATTN_HEADS, WINDOW, 2 * WINDOW), jnp.float32)),
        in_specs=[pl.BlockSpec(memory_space=pltpu.VMEM),
                  pl.BlockSpec(memory_space=pltpu.SMEM),
                  pl.BlockSpec(memory_space=pltpu.VMEM)],
        out_specs=(pl.BlockSpec(memory_space=pltpu.VMEM),
                   pl.BlockSpec(memory_space=pltpu.VMEM)),
        name="prep",
    )(lb_param.astype(jnp.float32), rel_bias.astype(jnp.float32), bucket)


TAIL_MERGE_CHUNK = 2048
INPROJ_TN = 512


def _ref_tile_of_our_tile():
    table = []
    for name in _OUR_ORDER:
        off, w = _REF_OFF[name]
        if name == "b_v":
            continue
        w = w + _REF_OFF["b_v"][1] if name == "b_k" else w
        assert off % INPROJ_TN == 0 and w % INPROJ_TN == 0
        table.extend(range(off // INPROJ_TN, (off + w) // INPROJ_TN))
    assert sorted(table) == list(range(N_IN // INPROJ_TN))
    return table


def _inproj_kernel(x_ref, w_ref, o_ref):
    o_ref[...] = jnp.dot(x_ref[...], w_ref[...].astype(jnp.bfloat16),
                         preferred_element_type=jnp.float32)


def _inproj(x_bf16, w_in, layer):
    m = x_bf16.shape[0]
    tm = min(2048, m)
    tn = INPROJ_TN
    assert m % tm == 0 and N_IN % tn == 0
    table = _ref_tile_of_our_tile()
    kv_ref = _REF_OFF["b_k"][0] // tn
    n_tiles = N_IN // tn
    assert table == [j if j < kv_ref else j + 1 for j in range(n_tiles - 1)] + [kv_ref]

    def w_map(i, j):
        return (layer, 0, jnp.where(j < kv_ref, j, jnp.where(j < n_tiles - 1, j + 1, kv_ref)))

    return pl.pallas_call(
        _inproj_kernel,
        out_shape=jax.ShapeDtypeStruct((m, N_IN), jnp.float32),
        grid=(m // tm, n_tiles),
        in_specs=[pl.BlockSpec((tm, D_MODEL), lambda i, j: (i, 0)),
                  pl.BlockSpec((None, D_MODEL, tn), w_map)],
        out_specs=pl.BlockSpec((tm, tn), lambda i, j: (i, j)),
        compiler_params=pltpu.CompilerParams(
            dimension_semantics=("parallel", "arbitrary"), vmem_limit_bytes=VMEM_LIMIT_BYTES),
        name="inproj",
    )(x_bf16, w_in)


_HGRN_LEVEL_SIZES = tuple(HGRN_CHUNK >> (i + 1) for i in range(HGRN_CHUNK.bit_length() - 1))
_HGRN_DIAG_LEVEL = len(_HGRN_LEVEL_SIZES)
_HGRN_N_SPANS = 2 + len(_HGRN_LEVEL_SIZES) - 1


def _hgrn_static_tables():
    c = HGRN_CHUNK
    j = np.arange(c)[None, :]
    t = np.arange(c)[:, None]
    blocks = [(j <= t), (j > t)]
    level = np.full((c, c), -1, np.int32)
    level[np.arange(c), np.arange(c)] = _HGRN_DIAG_LEVEL
    for li, s in enumerate(_HGRN_LEVEL_SIZES):
        r = (t // (2 * s)) * 2 * s + s - 1
        later = ((t // s) % 2) == 1
        if s > 1:
            blocks.append(np.where(later, (j > r) & (j <= t), (j > t) & (j <= r)))
        tt, ss = np.arange(c)[:, None], np.arange(c)[None, :]
        sel = (tt // (2 * s) == ss // (2 * s)) & ((tt // s) % 2 == 1) & ((ss // s) % 2 == 0)
        level[sel] = li
    span = np.concatenate(blocks, axis=0).astype(np.float32)
    return np.concatenate([span, span], axis=1), level


_NT_DIMS = (((1,), (1,)), ((), ()))
_TN_DIMS = (((0,), (0,)), ((), ()))


def _hgrn_operands(qf, k, f, e_blocks, odd_rows):
    c = HGRN_CHUNK
    bf = jnp.bfloat16
    ms = []
    for li, s in enumerate(_HGRN_LEVEL_SIZES):
        if s >= 8:
            qk_sel = jnp.concatenate(
                [(qf if (j % 2) else k)[j * s:(j + 1) * s] for j in range(c // s)], axis=0)
            ms.append((qk_sel * e_blocks[2 + li]).astype(bf))
        elif s > 1:
            ms.append((jnp.where(odd_rows[li], qf, k) * e_blocks[2 + li]).astype(bf))
        else:
            ms.append(jnp.where(odd_rows[li], qf * f, k).astype(bf))
    return dict(q=qf.astype(bf), k=k.astype(bf), levels=ms,
                qe=(qf * e_blocks[0]).astype(bf),
                kte=(k * e_blocks[1]).astype(bf),
                decay=e_blocks[0][c - 1:c, :])


def _hgrn_kernel(q_ref, f_ref, i_ref, g_ref, lb_ref, ng_ref, w_ref, lv_ref, o_ref, state_ref,
                 *, n_steps, n_heads):
    c, d, cu = HGRN_CHUNK, HGRN_HEAD_DIM, HGRN_CHUNK_UNROLL
    bf = jnp.bfloat16

    @pl.when(pl.program_id(2) == 0)
    def _():
        state_ref[...] = jnp.zeros_like(state_ref)

    lb = lb_ref[0]
    ng = ng_ref[0]
    row = lax.broadcasted_iota(jnp.int32, (c, d), 0)
    odd_rows = [((row // s) % 2) == 1 for s in _HGRN_LEVEL_SIZES]
    items = [(cc, hh) for cc in range(cu) for hh in range(n_heads)]

    def body(si, carry):
        rows = [pl.ds(pl.multiple_of((si * cu + cc) * c, c), c) for cc in range(cu)]
        level = lv_ref[...]
        fs, expos = [], []
        for cc in range(cu):
            f = lb + (1.0 - lb) * _sigmoid(f_ref[rows[cc], :])
            g = jnp.log(f)
            g_hi = g.astype(bf)
            g_lo = (g - g_hi.astype(jnp.float32)).astype(bf)
            fs.append(f)
            expos.append(jnp.dot(w_ref[...], jnp.concatenate([g_hi, g_lo], axis=0),
                                 preferred_element_type=jnp.float32))
        ops, vs = {}, {}
        for cc, hh in items:
            hs = slice(hh * d, (hh + 1) * d)
            qf = _silu(q_ref[rows[cc], hs]) * (d ** -0.5)
            f = fs[cc][:, hs]
            e_blocks = [jnp.exp(expos[cc][j * c:(j + 1) * c, hs]) for j in range(_HGRN_N_SPANS)]
            ops[cc, hh] = _hgrn_operands(qf, 1.0 - f, f, e_blocks, odd_rows)
            vs[cc, hh] = i_ref[rows[cc], hs].astype(bf)
        upds = {it: lax.dot_general(vs[it], ops[it]["kte"], _TN_DIMS,
                                    preferred_element_type=jnp.float32) for it in items}
        inter = {}
        for hh in range(n_heads):
            state_t = state_ref[hh]
            for cc in range(cu):
                inter[cc, hh] = lax.dot_general(ops[cc, hh]["qe"], state_t.astype(bf), _NT_DIMS,
                                                preferred_element_type=jnp.float32)
                state_t = state_t * ops[cc, hh]["decay"] + upds[cc, hh]
            state_ref[hh] = state_t
        scores = {}
        for it in items:
            op = ops[it]
            sc = lax.dot_general(op["q"], op["k"], _NT_DIMS, preferred_element_type=jnp.float32)
            sc = jnp.where(level == _HGRN_DIAG_LEVEL, sc, 0.0)
            for li, m in enumerate(op["levels"]):
                p = lax.dot_general(m, m, _NT_DIMS, preferred_element_type=jnp.float32)
                sc = jnp.where(level == li, p, sc)
            scores[it] = sc.astype(bf)
        for cc, hh in items:
            hs = slice(hh * d, (hh + 1) * d)
            o = inter[cc, hh] + jnp.dot(scores[cc, hh], vs[cc, hh],
                                        preferred_element_type=jnp.float32)
            o = o * lax.rsqrt(jnp.mean(o * o, axis=-1, keepdims=True) + RMS_EPS) * ng[:, hs]
            o_ref[rows[cc], hs] = (o * _silu(g_ref[rows[cc], hs])).astype(o_ref.dtype)
        return carry

    lax.fori_loop(0, n_steps, body, 0)


def _hgrn(u, lower_l, norm_g_l, bsz, t):
    tc = min(512, t)
    assert t % tc == 0 and tc % (HGRN_CHUNK * HGRN_CHUNK_UNROLL) == 0
    nt = t // tc
    hb = HGRN_HEADS_PER_STEP
    wd = hb * HGRN_HEAD_DIM
    span_w, level = _hgrn_static_tables()

    def col_spec(name):
        base = _OUR_OFF[name] // wd
        return pl.BlockSpec((tc, wd), lambda b, h, i: (b * nt + i, base + h))

    vec_spec = pl.BlockSpec((1, 1, wd), lambda b, h, i: (h, 0, 0))
    return pl.pallas_call(
        functools.partial(_hgrn_kernel, n_steps=tc // (HGRN_CHUNK * HGRN_CHUNK_UNROLL), n_heads=hb),
        out_shape=jax.ShapeDtypeStruct((bsz * t, HGRN_WIDTH), jnp.bfloat16),
        grid=(bsz, HGRN_HEADS // hb, nt),
        in_specs=[col_spec("a_q"), col_spec("a_f"), col_spec("a_i"), col_spec("a_g"),
                  vec_spec, vec_spec,
                  pl.BlockSpec(span_w.shape, lambda b, h, i: (0, 0)),
                  pl.BlockSpec(level.shape, lambda b, h, i: (0, 0))],
        out_specs=pl.BlockSpec((tc, wd), lambda b, h, i: (b * nt + i, h)),
        scratch_shapes=[pltpu.VMEM((hb, HGRN_HEAD_DIM, HGRN_HEAD_DIM), jnp.float32)],
        compiler_params=pltpu.CompilerParams(
            dimension_semantics=("parallel", "parallel", "arbitrary"),
            vmem_limit_bytes=VMEM_LIMIT_BYTES),
        name="hgrn",
    )(u, u, u, u, lower_l.reshape(HGRN_HEADS // hb, 1, wd), norm_g_l.reshape(HGRN_HEADS // hb, 1, wd),
      jnp.asarray(span_w, dtype=jnp.bfloat16), jnp.asarray(level))


def _attn_kernel(sink_ref, q_ref, kvp_ref, kvc_ref, g_ref, bias_ref, o_ref):
    dh = ATTN_HEAD_DIM
    bf = jnp.bfloat16
    kv_prev = kvp_ref[...]
    kv_cur = kvc_ref[...]
    vws, scores = [], []
    for kvh in range(ATTN_KV_HEADS):
        ks = slice(kvh * dh, (kvh + 1) * dh)
        vs = slice(KV_WIDTH + kvh * dh, KV_WIDTH + (kvh + 1) * dh)
        kw = jnp.concatenate([kv_prev[:, ks], kv_cur[:, ks]], axis=0).astype(bf)
        vws.append(jnp.concatenate([kv_prev[:, vs], kv_cur[:, vs]], axis=0).astype(bf))
        for gi in range(ATTN_GROUP):
            h = kvh * ATTN_GROUP + gi
            q = (q_ref[:, h * dh:(h + 1) * dh] * (dh ** -0.5)).astype(bf)
            scores.append(lax.dot_general(q, kw, (((1,), (1,)), ((), ())),
                                          preferred_element_type=jnp.float32) + bias_ref[h])
    probs, rdenoms = [], []
    for h, s in enumerate(scores):
        sink = sink_ref[h]
        m = jnp.maximum(jnp.max(s, axis=-1, keepdims=True), sink)
        p = jnp.exp(s - m)
        rdenoms.append(1.0 / (jnp.sum(p, axis=-1, keepdims=True) + jnp.exp(sink - m)))
        probs.append(p.astype(bf))
    outs = [jnp.dot(p, vws[h // ATTN_GROUP], preferred_element_type=jnp.float32) * rdenoms[h]
            for h, p in enumerate(probs)]
    o = jnp.concatenate(outs, axis=1)
    o_ref[...] = (o * _silu(g_ref[...])).astype(o_ref.dtype)


def _attn(u, bias, sinks_l, bsz, t):
    w = WINDOW
    nb = t // w
    q_blk = _OUR_OFF["b_q"] // ATTN_WIDTH
    g_blk = _OUR_OFF["b_g"] // ATTN_WIDTH
    kv_blk = _OUR_OFF["b_k"] // (2 * KV_WIDTH)
    assert _OUR_OFF["b_v"] == _OUR_OFF["b_k"] + KV_WIDTH
    return pl.pallas_call(
        _attn_kernel,
        out_shape=jax.ShapeDtypeStruct((bsz * t, ATTN_WIDTH), jnp.bfloat16),
        grid=(bsz, nb),
        in_specs=[pl.BlockSpec(memory_space=pltpu.SMEM),
                  pl.BlockSpec((w, ATTN_WIDTH), lambda b, n: (b * nb + n, q_blk)),
                  pl.BlockSpec((w, 2 * KV_WIDTH), lambda b, n: (b * nb + jnp.maximum(n - 1, 0), kv_blk)),
                  pl.BlockSpec((w, 2 * KV_WIDTH), lambda b, n: (b * nb + n, kv_blk)),
                  pl.BlockSpec((w, ATTN_WIDTH), lambda b, n: (b * nb + n, g_blk)),
                  pl.BlockSpec((None, ATTN_HEADS, w, 2 * w),
                               lambda b, n: (jnp.where(n == 0, 1, 0), 0, 0, 0))],
        out_specs=pl.BlockSpec((w, ATTN_WIDTH), lambda b, n: (b * nb + n, 0)),
        compiler_params=pltpu.CompilerParams(
            dimension_semantics=("parallel", "arbitrary"), vmem_limit_bytes=VMEM_LIMIT_BYTES),
        name="attn",
    )(sinks_l.astype(jnp.float32), u, u, u, u, bias)


def _tail_kernel(za_ref, zb_ref, cb_ref, cc_ref, cx_ref, cg_ref, hc_ref, hx_ref,
                 ma_ref, mb_ref, mc_ref, x_ref, wa_ref, wb_ref, wc_ref, wo_ref,
                 cw_ref, lng_ref, lnb_ref, xo_ref, xb_ref, *, tiles_per_seq):
    bf = jnp.bfloat16
    tm = x_ref.shape[0]
    h = cc_ref[...] * cx_ref[...]
    seq_start = (pl.program_id(0) % tiles_per_seq) == 0
    halo = jnp.where(seq_start, 0.0, hc_ref[...] * hx_ref[...])
    row = lax.broadcasted_iota(jnp.int32, (tm, CONV_WIDTH), 0)
    h1 = jnp.where(row >= 1, pltpu.roll(h, 1, 0), halo[7:8, :])
    h2 = jnp.where(row >= 2, pltpu.roll(h, 2, 0),
                   jnp.where(row == 1, halo[7:8, :], halo[6:7, :]))
    cw = cw_ref[...]
    y = cw[0:1, :] * h2 + cw[1:2, :] * h1 + cw[2:3, :] * h
    zc = (cb_ref[...] * y * _silu(cg_ref[...])).astype(bf)
    za, zb = za_ref[...], zb_ref[...]
    yo = None
    for c0 in range(0, D_MODEL, TAIL_MERGE_CHUNK):
        cs = slice(c0, c0 + TAIL_MERGE_CHUNK)
        ya = jnp.dot(za, wa_ref[:, cs], preferred_element_type=jnp.float32)
        yb = jnp.dot(zb, wb_ref[:, cs], preferred_element_type=jnp.float32)
        yc = jnp.dot(zc, wc_ref[:, cs], preferred_element_type=jnp.float32)
        merged = (_sigmoid(ma_ref[:, cs]) * ya + _sigmoid(mb_ref[:, cs]) * yb
                  + _sigmoid(mc_ref[:, cs]) * yc)
        part = jnp.dot(merged.astype(bf), wo_ref[cs, :], preferred_element_type=jnp.float32)
        yo = part if yo is None else yo + part
    r = ALPHA * x_ref[...] + yo
    mu = jnp.mean(r, axis=-1, keepdims=True)
    rc = r - mu
    var = jnp.mean(rc * rc, axis=-1, keepdims=True)
    xn = rc * lax.rsqrt(var + LN_EPS) * lng_ref[...] + lnb_ref[...]
    xo_ref[...] = xn
    xb_ref[...] = xn.astype(bf)


def _tail(za, zb, u, x, wa, wb, wc, wo, conv_w_l, ln_g_l, ln_b_l, t, layer):
    m = x.shape[0]
    tm = min(256, t)
    assert t % tm == 0
    tiles_per_seq = t // tm
    cw = CONV_WIDTH

    def seg(name, width):
        blk = _OUR_OFF[name] // width
        return pl.BlockSpec((tm, width), lambda i: (i, blk))

    def halo(name):
        blk = _OUR_OFF[name] // cw
        return pl.BlockSpec((8, cw), lambda i: (jnp.maximum(i * (tm // 8) - 1, 0), blk))

    def const(shape):
        return pl.BlockSpec(shape, lambda i: (0,) * len(shape), pipeline_mode=pl.Buffered(1))

    def weight(rows):
        return pl.BlockSpec((None, rows, D_MODEL), lambda i: (layer, 0, 0),
                            pipeline_mode=pl.Buffered(1))

    return pl.pallas_call(
        functools.partial(_tail_kernel, tiles_per_seq=tiles_per_seq),
        out_shape=(jax.ShapeDtypeStruct((m, D_MODEL), jnp.float32),
                   jax.ShapeDtypeStruct((m, D_MODEL), jnp.bfloat16)),
        grid=(m // tm,),
        in_specs=[pl.BlockSpec((tm, HGRN_WIDTH), lambda i: (i, 0)),
                  pl.BlockSpec((tm, ATTN_WIDTH), lambda i: (i, 0)),
                  seg("c_b", cw), seg("c_c", cw), seg("c_x", cw), seg("c_g", cw),
                  halo("c_c"), halo("c_x"),
                  seg("m_a", D_MODEL), seg("m_b", D_MODEL), seg("m_c", D_MODEL),
                  pl.BlockSpec((tm, D_MODEL), lambda i: (i, 0)),
                  weight(HGRN_WIDTH), weight(ATTN_WIDTH), weight(CONV_WIDTH), weight(D_MODEL),
                  const((CONV_K, cw)), const((1, D_MODEL)), const((1, D_MODEL))],
        out_specs=(pl.BlockSpec((tm, D_MODEL), lambda i: (i, 0)),
                   pl.BlockSpec((tm, D_MODEL), lambda i: (i, 0))),
        compiler_params=pltpu.CompilerParams(
            dimension_semantics=("parallel",), vmem_limit_bytes=VMEM_LIMIT_BYTES),
        name="tail",
    )(za, zb, u, u, u, u, u, u, u, u, u, x, wa, wb, wc, wo,
      conv_w_l, ln_g_l.reshape(1, D_MODEL), ln_b_l.reshape(1, D_MODEL))


def kernel(x, w_in, w_proj_hgrn, w_proj_attn, w_proj_conv, w_out, lb_param, hgrn_norm_g,
           attn_sinks, conv_w, rel_bias, ln_g, ln_b):
    bsz, t, d = x.shape
    assert d == D_MODEL and t % WINDOW == 0
    bf = jnp.bfloat16
    lower, bias = _prep(lb_param, rel_bias)
    w_in = w_in.astype(jnp.float32)
    wa_b, wb_b, wc_b, wo_b = (w.astype(bf) for w in (w_proj_hgrn, w_proj_attn, w_proj_conv, w_out))
    xf = x.reshape(bsz * t, d).astype(jnp.float32)
    xb = xf.astype(bf)
    for l in range(DEPTH):
        u = _inproj(xb, w_in, l)
        za = _hgrn(u, lower[l], hgrn_norm_g[l].astype(jnp.float32), bsz, t)
        zb = _attn(u, bias, attn_sinks[l], bsz, t)
        xf, xb = _tail(za, zb, u, xf, wa_b, wb_b, wc_b, wo_b,
                       conv_w[l].astype(jnp.float32), ln_g[l].astype(jnp.float32),
                       ln_b[l].astype(jnp.float32), t, l)
    return xf.reshape(bsz, t, d).astype(x.dtype)
```

```python
import functools
import math

import numpy as np
import jax
import jax.numpy as jnp
from jax import lax
from jax.experimental import pallas as pl
from jax.experimental.pallas import tpu as pltpu

D_MODEL = 2048
DEPTH = 4
HGRN_WIDTH = 1024
HGRN_HEAD_DIM = 128
HGRN_HEADS = HGRN_WIDTH // HGRN_HEAD_DIM
ATTN_HEAD_DIM = 64
ATTN_HEADS = 16
ATTN_KV_HEADS = 4
ATTN_GROUP = ATTN_HEADS // ATTN_KV_HEADS
ATTN_WIDTH = ATTN_HEADS * ATTN_HEAD_DIM
KV_WIDTH = ATTN_KV_HEADS * ATTN_HEAD_DIM
WINDOW = 128
CONV_WIDTH = 1024
CONV_K = 3
N_BUCKETS = 32
MAX_DISTANCE = 128
ALPHA = (2.0 * DEPTH) ** 0.25
LN_EPS = 1e-5
RMS_EPS = 1e-6
NEG_BIG = -1e30

_REF_SEGMENTS = (
    ("a_q", 1024), ("a_f", 1024), ("a_i", 1024), ("a_g", 1024),
    ("b_q", 1024), ("b_k", 256), ("b_v", 256), ("b_g", 1024),
    ("c_b", 1024), ("c_c", 1024), ("c_x", 1024), ("c_g", 1024),
    ("m_a", 2048), ("m_b", 2048), ("m_c", 2048),
)
_OUR_ORDER = ("a_q", "a_f", "a_i", "a_g", "b_q", "b_g", "c_b", "c_c", "c_x", "c_g",
              "m_a", "m_b", "m_c", "b_k", "b_v")
N_IN = sum(w for _, w in _REF_SEGMENTS)


def _segment_offsets():
    ref_off, off = {}, 0
    for name, w in _REF_SEGMENTS:
        ref_off[name] = (off, w)
        off += w
    our_off, off = {}, 0
    for name in _OUR_ORDER:
        w = ref_off[name][1]
        assert off % w == 0
        our_off[name] = off
        off += w
    return ref_off, our_off


_REF_OFF, _OUR_OFF = _segment_offsets()

LANES = 128
VMEM_LIMIT_BYTES = 56 * 1024 * 1024

HGRN_CHUNK = 64
HGRN_HEADS_PER_STEP = 8
HGRN_CHUNK_UNROLL = 4
ATTN_BLOCKS_PER_STEP = 1


def _t5_bucket_table():
    i = np.arange(WINDOW)[:, None]
    j = np.arange(2 * WINDOW)[None, :]
    rel = np.clip(WINDOW + i - j, 0, WINDOW - 1)
    max_exact = N_BUCKETS // 2
    logd = (np.log(np.maximum(rel, 1).astype(np.float32) / np.float32(max_exact))
            / np.float32(math.log(MAX_DISTANCE / max_exact))).astype(np.float32)
    large = max_exact + (logd * np.float32(N_BUCKETS - max_exact)).astype(np.int32)
    large = np.minimum(large, N_BUCKETS - 1)
    return np.where(rel < max_exact, rel, large).astype(np.int32)


def _sigmoid(x):
    return 1.0 / (1.0 + jnp.exp(-x))


def _silu(x):
    return x * _sigmoid(x)


def _prep_kernel(lb_ref, relb_ref, bucket_ref, lower_ref, bias_ref):
    lb = lb_ref[...]
    m = jnp.max(lb, axis=0, keepdims=True)
    e = jnp.exp(lb - m)
    soft = e / jnp.sum(e, axis=0, keepdims=True)
    acc = jnp.zeros_like(soft[0:1])
    for l in range(DEPTH):
        acc = acc + soft[l:l + 1]
        lower_ref[l:l + 1, :] = acc - soft[0:1]
    bucket = bucket_ref[...]
    i = lax.broadcasted_iota(jnp.int32, (WINDOW, 2 * WINDOW), 0)
    j = lax.broadcasted_iota(jnp.int32, (WINDOW, 2 * WINDOW), 1)
    rel = WINDOW + i - j
    band = (rel >= 0) & (rel < WINDOW)
    for h in range(ATTN_HEADS):
        acc_b = jnp.zeros((WINDOW, 2 * WINDOW), jnp.float32)
        for bk in range(N_BUCKETS):
            acc_b = jnp.where(bucket == bk, relb_ref[bk, h], acc_b)
        bias_ref[0, h] = jnp.where(band, acc_b, NEG_BIG)
        bias_ref[1, h] = jnp.where(band & (j >= WINDOW), acc_b, NEG_BIG)


def _prep(lb_param, rel_bias):
    bucket = jnp.asarray(_t5_bucket_table())
    return pl.pallas_call(
        _prep_kernel,
        out_shape=(jax.ShapeDtypeStruct((DEPTH, HGRN_WIDTH), jnp.float32),
                   jax.ShapeDtypeStruct((2, ATTN_HEADS, WINDOW, 2 * WINDOW), jnp.float32)),
        in_specs=[pl.BlockSpec(memory_space=pltpu.VMEM),
                  pl.BlockSpec(memory_space=pltpu.SMEM),
                  pl.BlockSpec(memory_space=pltpu.VMEM)],
        out_specs=(pl.BlockSpec(memory_space=pltpu.VMEM),
                   pl.BlockSpec(memory_space=pltpu.VMEM)),
        name="prep",
    )(lb_param.astype(jnp.float32), rel_bias.astype(jnp.float32), bucket)


TAIL_MERGE_CHUNK = 2048
INPROJ_TN = 512


def _ref_tile_of_our_tile():
    table = []
    for name in _OUR_ORDER:
        off, w = _REF_OFF[name]
        if name == "b_v":
            continue
        w = w + _REF_OFF["b_v"][1] if name == "b_k" else w
        assert off % INPROJ_TN == 0 and w % INPROJ_TN == 0
        table.extend(range(off // INPROJ_TN, (off + w) // INPROJ_TN))
    assert sorted(table) == list(range(N_IN // INPROJ_TN))
    return table


def _inproj_kernel(x_ref, w_ref, o_ref):
    o_ref[...] = jnp.dot(x_ref[...], w_ref[...].astype(jnp.bfloat16),
                         preferred_element_type=jnp.float32)


def _inproj(x_bf16, w_in, layer):
    m = x_bf16.shape[0]
    tm = min(2048, m)
    tn = INPROJ_TN
    assert m % tm == 0 and N_IN % tn == 0
    table = _ref_tile_of_our_tile()
    kv_ref = _REF_OFF["b_k"][0] // tn
    n_tiles = N_IN // tn
    assert table == [j if j < kv_ref else j + 1 for j in range(n_tiles - 1)] + [kv_ref]

    def w_map(i, j):
        return (layer, 0, jnp.where(j < kv_ref, j, jnp.where(j < n_tiles - 1, j + 1, kv_ref)))

    return pl.pallas_call(
        _inproj_kernel,
        out_shape=jax.ShapeDtypeStruct((m, N_IN), jnp.float32),
        grid=(m // tm, n_tiles),
        in_specs=[pl.BlockSpec((tm, D_MODEL), lambda i, j: (i, 0)),
                  pl.BlockSpec((None, D_MODEL, tn), w_map)],
        out_specs=pl.BlockSpec((tm, tn), lambda i, j: (i, j)),
        compiler_params=pltpu.CompilerParams(
            dimension_semantics=("parallel", "arbitrary"), vmem_limit_bytes=VMEM_LIMIT_BYTES),
        name="inproj",
    )(x_bf16, w_in)


_HGRN_LEVEL_SIZES = tuple(HGRN_CHUNK >> (i + 1) for i in range(HGRN_CHUNK.bit_length() - 1))
_HGRN_DIAG_LEVEL = len(_HGRN_LEVEL_SIZES)
_HGRN_N_SPANS = 2 + len(_HGRN_LEVEL_SIZES) - 1


def _hgrn_static_tables():
    c = HGRN_CHUNK
    j = np.arange(c)[None, :]
    t = np.arange(c)[:, None]
    blocks = [(j <= t), (j > t)]
    level = np.full((c, c), -1, np.int32)
    level[np.arange(c), np.arange(c)] = _HGRN_DIAG_LEVEL
    for li, s in enumerate(_HGRN_LEVEL_SIZES):
        r = (t // (2 * s)) * 2 * s + s - 1
        later = ((t // s) % 2) == 1
        if s > 1:
            blocks.append(np.where(later, (j > r) & (j <= t), (j > t) & (j <= r)))
        tt, ss = np.arange(c)[:, None], np.arange(c)[None, :]
        sel = (tt // (2 * s) == ss // (2 * s)) & ((tt // s) % 2 == 1) & ((ss // s) % 2 == 0)
        level[sel] = li
    span = np.concatenate(blocks, axis=0).astype(np.float32)
    return np.concatenate([span, span], axis=1), level


_NT_DIMS = (((1,), (1,)), ((), ()))
_TN_DIMS = (((0,), (0,)), ((), ()))


def _hgrn_operands(qf, k, f, e_blocks, odd_rows):
    c = HGRN_CHUNK
    bf = jnp.bfloat16
    ms = []
    for li, s in enumerate(_HGRN_LEVEL_SIZES):
        if s >= 8:
            qk_sel = jnp.concatenate(
                [(qf if (j % 2) else k)[j * s:(j + 1) * s] for j in range(c // s)], axis=0)
            ms.append((qk_sel * e_blocks[2 + li]).astype(bf))
        elif s > 1:
            ms.append((jnp.where(odd_rows[li], qf, k) * e_blocks[2 + li]).astype(bf))
        else:
            ms.append(jnp.where(odd_rows[li], qf * f, k).astype(bf))
    return dict(q=qf.astype(bf), k=k.astype(bf), levels=ms,
                qe=(qf * e_blocks[0]).astype(bf),
                kte=(k * e_blocks[1]).astype(bf),
                decay=e_blocks[0][c - 1:c, :])


def _hgrn_kernel(q_ref, f_ref, i_ref, g_ref, lb_ref, ng_ref, w_ref, lv_ref, o_ref, state_ref,
                 *, n_steps, n_heads):
    c, d, cu = HGRN_CHUNK, HGRN_HEAD_DIM, HGRN_CHUNK_UNROLL
    bf = jnp.bfloat16

    @pl.when(pl.program_id(2) == 0)
    def _():
        state_ref[...] = jnp.zeros_like(state_ref)

    lb = lb_ref[0]
    ng = ng_ref[0]
    row = lax.broadcasted_iota(jnp.int32, (c, d), 0)
    odd_rows = [((row // s) % 2) == 1 for s in _HGRN_LEVEL_SIZES]
    items = [(cc, hh) for cc in range(cu) for hh in range(n_heads)]

    def body(si, carry):
        rows = [pl.ds(pl.multiple_of((si * cu + cc) * c, c), c) for cc in range(cu)]
        level = lv_ref[...]
        fs, expos = [], []
        for cc in range(cu):
            f = lb + (1.0 - lb) * _sigmoid(f_ref[rows[cc], :])
            g = jnp.log(f)
            g_hi = g.astype(bf)
            g_lo = (g - g_hi.astype(jnp.float32)).astype(bf)
            fs.append(f)
            expos.append(jnp.dot(w_ref[...], jnp.concatenate([g_hi, g_lo], axis=0),
                                 preferred_element_type=jnp.float32))
        ops, vs = {}, {}
        for cc, hh in items:
            hs = slice(hh * d, (hh + 1) * d)
            qf = _silu(q_ref[rows[cc], hs]) * (d ** -0.5)
            f = fs[cc][:, hs]
            e_blocks = [jnp.exp(expos[cc][j * c:(j + 1) * c, hs]) for j in range(_HGRN_N_SPANS)]
            ops[cc, hh] = _hgrn_operands(qf, 1.0 - f, f, e_blocks, odd_rows)
            vs[cc, hh] = i_ref[rows[cc], hs].astype(bf)
        upds = {it: lax.dot_general(vs[it], ops[it]["kte"], _TN_DIMS,
                                    preferred_element_type=jnp.float32) for it in items}
        inter = {}
        for hh in range(n_heads):
            state_t = state_ref[hh]
            for cc in range(cu):
                inter[cc, hh] = lax.dot_general(ops[cc, hh]["qe"], state_t.astype(bf), _NT_DIMS,
                                                preferred_element_type=jnp.float32)
                state_t = state_t * ops[cc, hh]["decay"] + upds[cc, hh]
            state_ref[hh] = state_t
        scores = {}
        for it in items:
            op = ops[it]
            sc = lax.dot_general(op["q"], op["k"], _NT_DIMS, preferred_element_type=jnp.float32)
            sc = jnp.where(level == _HGRN_DIAG_LEVEL, sc, 0.0)
            for li, m in enumerate(op["levels"]):
                p = lax.dot_general(m, m, _NT_DIMS, preferred_element_type=jnp.float32)
                sc = jnp.where(level == li, p, sc)
            scores[it] = sc.astype(bf)
        for cc, hh in items:
            hs = slice(hh * d, (hh + 1) * d)
            o = inter[cc, hh] + jnp.dot(scores[cc, hh], vs[cc, hh],
                                        preferred_element_type=jnp.float32)
            o = o * lax.rsqrt(jnp.mean(o * o, axis=-1, keepdims=True) + RMS_EPS) * ng[:, hs]
            o_ref[rows[cc], hs] = (o * _silu(g_ref[rows[cc], hs])).astype(o_ref.dtype)
        return carry

    lax.fori_loop(0, n_steps, body, 0)


def _hgrn(u, lower_l, norm_g_l, bsz, t):
    tc = min(512, t)
    assert t % tc == 0 and tc % (HGRN_CHUNK * HGRN_CHUNK_UNROLL) == 0
    nt = t // tc
    hb = HGRN_HEADS_PER_STEP
    wd = hb * HGRN_HEAD_DIM
    span_w, level = _hgrn_static_tables()

    def col_spec(name):
        base = _OUR_OFF[name] // wd
        return pl.BlockSpec((tc, wd), lambda b, h, i: (b * nt + i, base + h))

    vec_spec = pl.BlockSpec((1, 1, wd), lambda b, h, i: (h, 0, 0))
    return pl.pallas_call(
        functools.partial(_hgrn_kernel, n_steps=tc // (HGRN_CHUNK * HGRN_CHUNK_UNROLL), n_heads=hb),
        out_shape=jax.ShapeDtypeStruct((bsz * t, HGRN_WIDTH), jnp.bfloat16),
        grid=(bsz, HGRN_HEADS // hb, nt),
        in_specs=[col_spec("a_q"), col_spec("a_f"), col_spec("a_i"), col_spec("a_g"),
                  vec_spec, vec_spec,
                  pl.BlockSpec(span_w.shape, lambda b, h, i: (0, 0)),
                  pl.BlockSpec(level.shape, lambda b, h, i: (0, 0))],
        out_specs=pl.BlockSpec((tc, wd), lambda b, h, i: (b * nt + i, h)),
        scratch_shapes=[pltpu.VMEM((hb, HGRN_HEAD_DIM, HGRN_HEAD_DIM), jnp.float32)],
        compiler_params=pltpu.CompilerParams(
            dimension_semantics=("parallel", "parallel", "arbitrary"),
            vmem_limit_bytes=VMEM_LIMIT_BYTES),
        name="hgrn",
    )(u, u, u, u, lower_l.reshape(HGRN_HEADS // hb, 1, wd), norm_g_l.reshape(HGRN_HEADS // hb, 1, wd),
      jnp.asarray(span_w, dtype=jnp.bfloat16), jnp.asarray(level))


def _attn_kernel(sink_ref, q_ref, kvp_ref, kvc_ref, g_ref, bias_ref, o_ref):
    w, dh = WINDOW, ATTN_HEAD_DIM
    bf = jnp.bfloat16
    kv = jnp.concatenate([kvp_ref[...], kvc_ref[...]], axis=0)
    first_table = jnp.where(pl.program_id(1) == 0, 1, 0)
    items = [(sb, h) for sb in range(ATTN_BLOCKS_PER_STEP) for h in range(ATTN_HEADS)]
    kws, vws = {}, {}
    for sb in range(ATTN_BLOCKS_PER_STEP):
        for kvh in range(ATTN_KV_HEADS):
            rows = slice(sb * w, (sb + 2) * w)
            kws[sb, kvh] = kv[rows, kvh * dh:(kvh + 1) * dh].astype(bf)
            vws[sb, kvh] = kv[rows, KV_WIDTH + kvh * dh:KV_WIDTH + (kvh + 1) * dh].astype(bf)
    scores = {}
    for sb, h in items:
        q = (q_ref[sb * w:(sb + 1) * w, h * dh:(h + 1) * dh] * (dh ** -0.5)).astype(bf)
        bias = bias_ref[first_table, h] if sb == 0 else bias_ref[0, h]
        scores[sb, h] = lax.dot_general(q, kws[sb, h // ATTN_GROUP], (((1,), (1,)), ((), ())),
                                        preferred_element_type=jnp.float32) + bias
    probs, rdenoms = {}, {}
    for sb, h in items:
        s = scores[sb, h]
        sink = sink_ref[h]
        m = jnp.maximum(jnp.max(s, axis=-1, keepdims=True), sink)
        p = jnp.exp(s - m)
        rdenoms[sb, h] = 1.0 / (jnp.sum(p, axis=-1, keepdims=True) + jnp.exp(sink - m))
        probs[sb, h] = p.astype(bf)
    for sb in range(ATTN_BLOCKS_PER_STEP):
        outs = [jnp.dot(probs[sb, h], vws[sb, h // ATTN_GROUP], preferred_element_type=jnp.float32)
                * rdenoms[sb, h] for h in range(ATTN_HEADS)]
        o = jnp.concatenate(outs, axis=1)
        rs = slice(sb * w, (sb + 1) * w)
        o_ref[rs, :] = (o * _silu(g_ref[rs, :])).astype(o_ref.dtype)


def _attn(u, bias, sinks_l, bsz, t):
    w = WINDOW
    nq = ATTN_BLOCKS_PER_STEP
    assert t % (nq * w) == 0
    nb = t // (nq * w)
    q_blk = _OUR_OFF["b_q"] // ATTN_WIDTH
    g_blk = _OUR_OFF["b_g"] // ATTN_WIDTH
    kv_blk = _OUR_OFF["b_k"] // (2 * KV_WIDTH)
    assert _OUR_OFF["b_v"] == _OUR_OFF["b_k"] + KV_WIDTH
    return pl.pallas_call(
        _attn_kernel,
        out_shape=jax.ShapeDtypeStruct((bsz * t, ATTN_WIDTH), jnp.bfloat16),
        grid=(bsz, nb),
        in_specs=[pl.BlockSpec(memory_space=pltpu.SMEM),
                  pl.BlockSpec((nq * w, ATTN_WIDTH), lambda b, n: (b * nb + n, q_blk)),
                  pl.BlockSpec((w, 2 * KV_WIDTH),
                               lambda b, n: (nq * (b * nb + n) - jnp.where(n == 0, 0, 1), kv_blk)),
                  pl.BlockSpec((nq * w, 2 * KV_WIDTH), lambda b, n: (b * nb + n, kv_blk)),
                  pl.BlockSpec((nq * w, ATTN_WIDTH), lambda b, n: (b * nb + n, g_blk)),
                  pl.BlockSpec((2, ATTN_HEADS, w, 2 * w), lambda b, n: (0, 0, 0, 0))],
        out_specs=pl.BlockSpec((nq * w, ATTN_WIDTH), lambda b, n: (b * nb + n, 0)),
        compiler_params=pltpu.CompilerParams(
            dimension_semantics=("parallel", "arbitrary"), vmem_limit_bytes=VMEM_LIMIT_BYTES),
        name="attn",
    )(sinks_l.astype(jnp.float32), u, u, u, u, bias)


def _tail_kernel(za_ref, zb_ref, cb_ref, cc_ref, cx_ref, cg_ref, hc_ref, hx_ref,
                 ma_ref, mb_ref, mc_ref, x_ref, wa_ref, wb_ref, wc_ref, wo_ref,
                 cw_ref, lng_ref, lnb_ref, xo_ref, xb_ref, *, tiles_per_seq):
    bf = jnp.bfloat16
    tm = x_ref.shape[0]
    h = cc_ref[...] * cx_ref[...]
    seq_start = (pl.program_id(0) % tiles_per_seq) == 0
    halo = jnp.where(seq_start, 0.0, hc_ref[...] * hx_ref[...])
    row = lax.broadcasted_iota(jnp.int32, (tm, CONV_WIDTH), 0)
    h1 = jnp.where(row >= 1, pltpu.roll(h, 1, 0), halo[7:8, :])
    h2 = jnp.where(row >= 2, pltpu.roll(h, 2, 0),
                   jnp.where(row == 1, halo[7:8, :], halo[6:7, :]))
    cw = cw_ref[...]
    y = cw[0:1, :] * h2 + cw[1:2, :] * h1 + cw[2:3, :] * h
    zc = (cb_ref[...] * y * _silu(cg_ref[...])).astype(bf)
    za, zb = za_ref[...], zb_ref[...]
    yo = None
    for c0 in range(0, D_MODEL, TAIL_MERGE_CHUNK):
        cs = slice(c0, c0 + TAIL_MERGE_CHUNK)
        ya = jnp.dot(za, wa_ref[:, cs], preferred_element_type=jnp.float32)
        yb = jnp.dot(zb, wb_ref[:, cs], preferred_element_type=jnp.float32)
        yc = jnp.dot(zc, wc_ref[:, cs], preferred_element_type=jnp.float32)
        merged = (_sigmoid(ma_ref[:, cs]) * ya + _sigmoid(mb_ref[:, cs]) * yb
                  + _sigmoid(mc_ref[:, cs]) * yc)
        part = jnp.dot(merged.astype(bf), wo_ref[cs, :], preferred_element_type=jnp.float32)
        yo = part if yo is None else yo + part
    r = ALPHA * x_ref[...] + yo
    mu = jnp.mean(r, axis=-1, keepdims=True)
    rc = r - mu
    var = jnp.mean(rc * rc, axis=-1, keepdims=True)
    xn = rc * lax.rsqrt(var + LN_EPS) * lng_ref[...] + lnb_ref[...]
    xo_ref[...] = xn
    xb_ref[...] = xn.astype(bf)


def _tail(za, zb, u, x, wa, wb, wc, wo, conv_w_l, ln_g_l, ln_b_l, t, layer):
    m = x.shape[0]
    tm = min(256, t)
    assert t % tm == 0
    tiles_per_seq = t // tm
    cw = CONV_WIDTH

    def seg(name, width):
        blk = _OUR_OFF[name] // width
        return pl.BlockSpec((tm, width), lambda i: (i, blk))

    def halo(name):
        blk = _OUR_OFF[name] // cw
        return pl.BlockSpec((8, cw), lambda i: (jnp.maximum(i * (tm // 8) - 1, 0), blk))

    def const(shape):
        return pl.BlockSpec(shape, lambda i: (0,) * len(shape), pipeline_mode=pl.Buffered(1))

    def weight(rows):
        return pl.BlockSpec((None, rows, D_MODEL), lambda i: (layer, 0, 0),
                            pipeline_mode=pl.Buffered(1))

    return pl.pallas_call(
        functools.partial(_tail_kernel, tiles_per_seq=tiles_per_seq),
        out_shape=(jax.ShapeDtypeStruct((m, D_MODEL), jnp.float32),
                   jax.ShapeDtypeStruct((m, D_MODEL), jnp.bfloat16)),
        grid=(m // tm,),
        in_specs=[pl.BlockSpec((tm, HGRN_WIDTH), lambda i: (i, 0)),
                  pl.BlockSpec((tm, ATTN_WIDTH), lambda i: (i, 0)),
                  seg("c_b", cw), seg("c_c", cw), seg("c_x", cw), seg("c_g", cw),
                  halo("c_c"), halo("c_x"),
                  seg("m_a", D_MODEL), seg("m_b", D_MODEL), seg("m_c", D_MODEL),
                  pl.BlockSpec((tm, D_MODEL), lambda i: (i, 0)),
                  weight(HGRN_WIDTH), weight(ATTN_WIDTH), weight(CONV_WIDTH), weight(D_MODEL),
                  const((CONV_K, cw)), const((1, D_MODEL)), const((1, D_MODEL))],
        out_specs=(pl.BlockSpec((tm, D_MODEL), lambda i: (i, 0)),
                   pl.BlockSpec((tm, D_MODEL), lambda i: (i, 0))),
        compiler_params=pltpu.CompilerParams(
            dimension_semantics=("parallel",), vmem_limit_bytes=VMEM_LIMIT_BYTES),
        name="tail",
    )(za, zb, u, u, u, u, u, u, u, u, u, x, wa, wb, wc, wo,
      conv_w_l, ln_g_l.reshape(1, D_MODEL), ln_b_l.reshape(1, D_MODEL))


def kernel(x, w_in, w_proj_hgrn, w_proj_attn, w_proj_conv, w_out, lb_param, hgrn_norm_g,
           attn_sinks, conv_w, rel_bias, ln_g, ln_b):
    bsz, t, d = x.shape
    assert d == D_MODEL and t % WINDOW == 0
    bf = jnp.bfloat16
    lower, bias = _prep(lb_param, rel_bias)
    w_in = w_in.astype(jnp.float32)
    wa_b, wb_b, wc_b, wo_b = (w.astype(bf) for w in (w_proj_hgrn, w_proj_attn, w_proj_conv, w_out))
    xf = x.reshape(bsz * t, d).astype(jnp.float32)
    xb = xf.astype(bf)
    for l in range(DEPTH):
        u = _inproj(xb, w_in, l)
        za = _hgrn(u, lower[l], hgrn_norm_g[l].astype(jnp.float32), bsz, t)
        zb = _attn(u, bias, attn_sinks[l], bsz, t)
        xf, xb = _tail(za, zb, u, xf, wa_b, wb_b, wc_b, wo_b,
                       conv_w[l].astype(jnp.float32), ln_g[l].astype(jnp.float32),
                       ln_b[l].astype(jnp.float32), t, l)
    return xf.reshape(bsz, t, d).astype(x.dtype)
```

```python
import functools
import math

import numpy as np
import jax
import jax.numpy as jnp
from jax import lax
from jax.experimental import pallas as pl
from jax.experimental.pallas import tpu as pltpu

D_MODEL = 2048
DEPTH = 4
HGRN_WIDTH = 1024
HGRN_HEAD_DIM = 128
HGRN_HEADS = HGRN_WIDTH // HGRN_HEAD_DIM
ATTN_HEAD_DIM = 64
ATTN_HEADS = 16
ATTN_KV_HEADS = 4
ATTN_GROUP = ATTN_HEADS // ATTN_KV_HEADS
ATTN_WIDTH = ATTN_HEADS * ATTN_HEAD_DIM
KV_WIDTH = ATTN_KV_HEADS * ATTN_HEAD_DIM
WINDOW = 128
CONV_WIDTH = 1024
CONV_K = 3
N_BUCKETS = 32
MAX_DISTANCE = 128
ALPHA = (2.0 * DEPTH) ** 0.25
LN_EPS = 1e-5
RMS_EPS = 1e-6
NEG_BIG = -1e30

_REF_SEGMENTS = (
    ("a_q", 1024), ("a_f", 1024), ("a_i", 1024), ("a_g", 1024),
    ("b_q", 1024), ("b_k", 256), ("b_v", 256), ("b_g", 1024),
    ("c_b", 1024), ("c_c", 1024), ("c_x", 1024), ("c_g", 1024),
    ("m_a", 2048), ("m_b", 2048), ("m_c", 2048),
)
_OUR_ORDER = ("a_q", "a_f", "a_i", "a_g", "b_q", "b_g", "c_b", "c_c", "c_x", "c_g",
              "m_a", "m_b", "m_c", "b_k", "b_v")
N_IN = sum(w for _, w in _REF_SEGMENTS)


def _segment_offsets():
    ref_off, off = {}, 0
    for name, w in _REF_SEGMENTS:
        ref_off[name] = (off, w)
        off += w
    our_off, off = {}, 0
    for name in _OUR_ORDER:
        w = ref_off[name][1]
        assert off % w == 0
        our_off[name] = off
        off += w
    return ref_off, our_off


_REF_OFF, _OUR_OFF = _segment_offsets()

LANES = 128
VMEM_LIMIT_BYTES = 56 * 1024 * 1024

HGRN_CHUNK = 64
HGRN_HEADS_PER_STEP = 4
HGRN_CHUNK_UNROLL = 4


def _t5_bucket_table():
    i = np.arange(WINDOW)[:, None]
    j = np.arange(2 * WINDOW)[None, :]
    rel = np.clip(WINDOW + i - j, 0, WINDOW - 1)
    max_exact = N_BUCKETS // 2
    logd = (np.log(np.maximum(rel, 1).astype(np.float32) / np.float32(max_exact))
            / np.float32(math.log(MAX_DISTANCE / max_exact))).astype(np.float32)
    large = max_exact + (logd * np.float32(N_BUCKETS - max_exact)).astype(np.int32)
    large = np.minimum(large, N_BUCKETS - 1)
    return np.where(rel < max_exact, rel, large).astype(np.int32)


def _sigmoid(x):
    return 1.0 / (1.0 + jnp.exp(-x))


def _silu(x):
    return x * _sigmoid(x)


def _prep_kernel(lb_ref, relb_ref, bucket_ref, lower_ref, bias_ref):
    lb = lb_ref[...]
    m = jnp.max(lb, axis=0, keepdims=True)
    e = jnp.exp(lb - m)
    soft = e / jnp.sum(e, axis=0, keepdims=True)
    acc = jnp.zeros_like(soft[0:1])
    for l in range(DEPTH):
        acc = acc + soft[l:l + 1]
        lower_ref[l:l + 1, :] = acc - soft[0:1]
    bucket = bucket_ref[...]
    i = lax.broadcasted_iota(jnp.int32, (WINDOW, 2 * WINDOW), 0)
    j = lax.broadcasted_iota(jnp.int32, (WINDOW, 2 * WINDOW), 1)
    rel = WINDOW + i - j
    band = (rel >= 0) & (rel < WINDOW)
    for h in range(ATTN_HEADS):
        acc_b = jnp.zeros((WINDOW, 2 * WINDOW), jnp.float32)
        for bk in range(N_BUCKETS):
            acc_b = jnp.where(bucket == bk, relb_ref[bk, h], acc_b)
        bias_ref[0, h] = jnp.where(band, acc_b, NEG_BIG)
        bias_ref[1, h] = jnp.where(band & (j >= WINDOW), acc_b, NEG_BIG)


def _prep(lb_param, rel_bias):
    bucket = jnp.asarray(_t5_bucket_table())
    return pl.pallas_call(
        _prep_kernel,
        out_shape=(jax.ShapeDtypeStruct((DEPTH, HGRN_WIDTH), jnp.float32),
                   jax.ShapeDtypeStruct((2, ATTN_HEADS, WINDOW, 2 * WINDOW), jnp.float32)),
        in_specs=[pl.BlockSpec(memory_space=pltpu.VMEM),
                  pl.BlockSpec(memory_space=pltpu.SMEM),
                  pl.BlockSpec(memory_space=pltpu.VMEM)],
        out_specs=(pl.BlockSpec(memory_space=pltpu.VMEM),
                   pl.BlockSpec(memory_space=pltpu.VMEM)),
        name="prep",
    )(lb_param.astype(jnp.float32), rel_bias.astype(jnp.float32), bucket)


TAIL_MERGE_CHUNK = 2048
INPROJ_TN = 512


def _ref_tile_of_our_tile():
    table = []
    for name in _OUR_ORDER:
        off, w = _REF_OFF[name]
        if name == "b_v":
            continue
        w = w + _REF_OFF["b_v"][1] if name == "b_k" else w
        assert off % INPROJ_TN == 0 and w % INPROJ_TN == 0
        table.extend(range(off // INPROJ_TN, (off + w) // INPROJ_TN))
    assert sorted(table) == list(range(N_IN // INPROJ_TN))
    return table


def _inproj_kernel(x_ref, w_ref, o_ref):
    o_ref[...] = jnp.dot(x_ref[...], w_ref[...].astype(jnp.bfloat16),
                         preferred_element_type=jnp.float32)


def _inproj(x_bf16, w_in, layer):
    m = x_bf16.shape[0]
    tm = min(4096, m)
    tn = INPROJ_TN
    assert m % tm == 0 and N_IN % tn == 0
    table = _ref_tile_of_our_tile()
    kv_ref = _REF_OFF["b_k"][0] // tn
    n_tiles = N_IN // tn
    assert table == [j if j < kv_ref else j + 1 for j in range(n_tiles - 1)] + [kv_ref]

    def w_map(i, j):
        return (layer, 0, jnp.where(j < kv_ref, j, jnp.where(j < n_tiles - 1, j + 1, kv_ref)))

    return pl.pallas_call(
        _inproj_kernel,
        out_shape=jax.ShapeDtypeStruct((m, N_IN), jnp.float32),
        grid=(m // tm, n_tiles),
        in_specs=[pl.BlockSpec((tm, D_MODEL), lambda i, j: (i, 0), pipeline_mode=pl.Buffered(1)),
                  pl.BlockSpec((None, D_MODEL, tn), w_map)],
        out_specs=pl.BlockSpec((tm, tn), lambda i, j: (i, j)),
        compiler_params=pltpu.CompilerParams(
            dimension_semantics=("parallel", "arbitrary"), vmem_limit_bytes=VMEM_LIMIT_BYTES),
        name="inproj",
    )(x_bf16, w_in)


_HGRN_LEVEL_SIZES = tuple(HGRN_CHUNK >> (i + 1) for i in range(HGRN_CHUNK.bit_length() - 1))
_HGRN_DIAG_LEVEL = len(_HGRN_LEVEL_SIZES)
_HGRN_N_SPANS = 2 + len(_HGRN_LEVEL_SIZES) - 1


def _hgrn_static_tables():
    c = HGRN_CHUNK
    j = np.arange(c)[None, :]
    t = np.arange(c)[:, None]
    blocks = [(j <= t), (j > t)]
    level = np.full((c, c), -1, np.int32)
    level[np.arange(c), np.arange(c)] = _HGRN_DIAG_LEVEL
    for li, s in enumerate(_HGRN_LEVEL_SIZES):
        r = (t // (2 * s)) * 2 * s + s - 1
        later = ((t // s) % 2) == 1
        if s > 1:
            blocks.append(np.where(later, (j > r) & (j <= t), (j > t) & (j <= r)))
        tt, ss = np.arange(c)[:, None], np.arange(c)[None, :]
        sel = (tt // (2 * s) == ss // (2 * s)) & ((tt // s) % 2 == 1) & ((ss // s) % 2 == 0)
        level[sel] = li
    span = np.concatenate(blocks, axis=0).astype(np.float32)
    return np.concatenate([span, span], axis=1), level


_NT_DIMS = (((1,), (1,)), ((), ()))
_TN_DIMS = (((0,), (0,)), ((), ()))


def _hgrn_operands(qf, k, f, e_blocks, odd_rows):
    c = HGRN_CHUNK
    bf = jnp.bfloat16
    ms = []
    for li, s in enumerate(_HGRN_LEVEL_SIZES):
        if s >= 8:
            qk_sel = jnp.concatenate(
                [(qf if (j % 2) else k)[j * s:(j + 1) * s] for j in range(c // s)], axis=0)
            ms.append((qk_sel * e_blocks[2 + li]).astype(bf))
        elif s > 1:
            ms.append((jnp.where(odd_rows[li], qf, k) * e_blocks[2 + li]).astype(bf))
        else:
            ms.append(jnp.where(odd_rows[li], qf * f, k).astype(bf))
    return dict(q=qf.astype(bf), k=k.astype(bf), levels=ms,
                qe=(qf * e_blocks[0]).astype(bf),
                kte=(k * e_blocks[1]).astype(bf),
                decay=e_blocks[0][c - 1:c, :])


def _hgrn_kernel(q_ref, f_ref, i_ref, g_ref, lb_ref, ng_ref, w_ref, lv_ref, o_ref, state_ref,
                 *, n_steps, n_heads):
    c, d, cu = HGRN_CHUNK, HGRN_HEAD_DIM, HGRN_CHUNK_UNROLL
    bf = jnp.bfloat16

    @pl.when(pl.program_id(2) == 0)
    def _():
        state_ref[...] = jnp.zeros_like(state_ref)

    lb = lb_ref[0]
    ng = ng_ref[0]
    row = lax.broadcasted_iota(jnp.int32, (c, d), 0)
    odd_rows = [((row // s) % 2) == 1 for s in _HGRN_LEVEL_SIZES]
    items = [(cc, hh) for cc in range(cu) for hh in range(n_heads)]

    def body(si, carry):
        rows = [pl.ds(pl.multiple_of((si * cu + cc) * c, c), c) for cc in range(cu)]
        level = lv_ref[...]
        fs, expos = [], []
        for cc in range(cu):
            f = lb + (1.0 - lb) * _sigmoid(f_ref[rows[cc], :])
            g = jnp.log(f)
            g_hi = g.astype(bf)
            g_lo = (g - g_hi.astype(jnp.float32)).astype(bf)
            fs.append(f)
            expos.append(jnp.dot(w_ref[...], jnp.concatenate([g_hi, g_lo], axis=0),
                                 preferred_element_type=jnp.float32))
        ops, vs = {}, {}
        for cc, hh in items:
            hs = slice(hh * d, (hh + 1) * d)
            qf = _silu(q_ref[rows[cc], hs]) * (d ** -0.5)
            f = fs[cc][:, hs]
            e_blocks = [jnp.exp(expos[cc][j * c:(j + 1) * c, hs]) for j in range(_HGRN_N_SPANS)]
            ops[cc, hh] = _hgrn_operands(qf, 1.0 - f, f, e_blocks, odd_rows)
            vs[cc, hh] = i_ref[rows[cc], hs].astype(bf)
        upds = {it: lax.dot_general(vs[it], ops[it]["kte"], _TN_DIMS,
                                    preferred_element_type=jnp.float32) for it in items}
        inter = {}
        for hh in range(n_heads):
            state_t = state_ref[hh]
            for cc in range(cu):
                inter[cc, hh] = lax.dot_general(ops[cc, hh]["qe"], state_t.astype(bf), _NT_DIMS,
                                                preferred_element_type=jnp.float32)
                state_t = state_t * ops[cc, hh]["decay"] + upds[cc, hh]
            state_ref[hh] = state_t
        scores = {}
        for it in items:
            op = ops[it]
            sc = lax.dot_general(op["q"], op["k"], _NT_DIMS, preferred_element_type=jnp.float32)
            sc = jnp.where(level == _HGRN_DIAG_LEVEL, sc, 0.0)
            for li, m in enumerate(op["levels"]):
                p = lax.dot_general(m, m, _NT_DIMS, preferred_element_type=jnp.float32)
                sc = jnp.where(level == li, p, sc)
            scores[it] = sc.astype(bf)
        for cc, hh in items:
            hs = slice(hh * d, (hh + 1) * d)
            o = inter[cc, hh] + jnp.dot(scores[cc, hh], vs[cc, hh],
                                        preferred_element_type=jnp.float32)
            o = o * lax.rsqrt(jnp.mean(o * o, axis=-1, keepdims=True) + RMS_EPS) * ng[:, hs]
            o_ref[rows[cc], hs] = (o * _silu(g_ref[rows[cc], hs])).astype(o_ref.dtype)
        return carry

    lax.fori_loop(0, n_steps, body, 0)


def _hgrn(u, lower_l, norm_g_l, bsz, t):
    tc = min(512, t)
    assert t % tc == 0 and tc % (HGRN_CHUNK * HGRN_CHUNK_UNROLL) == 0
    nt = t // tc
    hb = HGRN_HEADS_PER_STEP
    wd = hb * HGRN_HEAD_DIM
    span_w, level = _hgrn_static_tables()

    def col_spec(name):
        base = _OUR_OFF[name] // wd
        return pl.BlockSpec((tc, wd), lambda b, h, i: (b * nt + i, base + h))

    vec_spec = pl.BlockSpec((1, 1, wd), lambda b, h, i: (h, 0, 0))
    return pl.pallas_call(
        functools.partial(_hgrn_kernel, n_steps=tc // (HGRN_CHUNK * HGRN_CHUNK_UNROLL), n_heads=hb),
        out_shape=jax.ShapeDtypeStruct((bsz * t, HGRN_WIDTH), jnp.bfloat16),
        grid=(bsz, HGRN_HEADS // hb, nt),
        in_specs=[col_spec("a_q"), col_spec("a_f"), col_spec("a_i"), col_spec("a_g"),
                  vec_spec, vec_spec,
                  pl.BlockSpec(span_w.shape, lambda b, h, i: (0, 0)),
                  pl.BlockSpec(level.shape, lambda b, h, i: (0, 0))],
        out_specs=pl.BlockSpec((tc, wd), lambda b, h, i: (b * nt + i, h)),
        scratch_shapes=[pltpu.VMEM((hb, HGRN_HEAD_DIM, HGRN_HEAD_DIM), jnp.float32)],
        compiler_params=pltpu.CompilerParams(
            dimension_semantics=("parallel", "parallel", "arbitrary"),
            vmem_limit_bytes=VMEM_LIMIT_BYTES),
        name="hgrn",
    )(u, u, u, u, lower_l.reshape(HGRN_HEADS // hb, 1, wd), norm_g_l.reshape(HGRN_HEADS // hb, 1, wd),
      jnp.asarray(span_w, dtype=jnp.bfloat16), jnp.asarray(level))


def _attn_kernel(sink_ref, q_ref, kvp_ref, kvc_ref, g_ref, bias_ref, o_ref):
    dh = ATTN_HEAD_DIM
    bf = jnp.bfloat16
    kv_prev = kvp_ref[...]
    kv_cur = kvc_ref[...]
    vws, scores = [], []
    for kvh in range(ATTN_KV_HEADS):
        ks = slice(kvh * dh, (kvh + 1) * dh)
        vs = slice(KV_WIDTH + kvh * dh, KV_WIDTH + (kvh + 1) * dh)
        kw = jnp.concatenate([kv_prev[:, ks], kv_cur[:, ks]], axis=0).astype(bf)
        vws.append(jnp.concatenate([kv_prev[:, vs], kv_cur[:, vs]], axis=0).astype(bf))
        for gi in range(ATTN_GROUP):
            h = kvh * ATTN_GROUP + gi
            q = (q_ref[:, h * dh:(h + 1) * dh] * (dh ** -0.5)).astype(bf)
            scores.append(lax.dot_general(q, kw, (((1,), (1,)), ((), ())),
                                          preferred_element_type=jnp.float32) + bias_ref[h])
    probs, rdenoms = [], []
    for h, s in enumerate(scores):
        sink = sink_ref[h]
        m = jnp.maximum(jnp.max(s, axis=-1, keepdims=True), sink)
        p = jnp.exp(s - m)
        rdenoms.append(1.0 / (jnp.sum(p, axis=-1, keepdims=True) + jnp.exp(sink - m)))
        probs.append(p.astype(bf))
    outs = [jnp.dot(p, vws[h // ATTN_GROUP], preferred_element_type=jnp.float32) * rdenoms[h]
            for h, p in enumerate(probs)]
    o = jnp.concatenate(outs, axis=1)
    o_ref[...] = (o * _silu(g_ref[...])).astype(o_ref.dtype)


def _attn(u, bias, sinks_l, bsz, t):
    w = WINDOW
    nb = t // w
    q_blk = _OUR_OFF["b_q"] // ATTN_WIDTH
    g_blk = _OUR_OFF["b_g"] // ATTN_WIDTH
    kv_blk = _OUR_OFF["b_k"] // (2 * KV_WIDTH)
    assert _OUR_OFF["b_v"] == _OUR_OFF["b_k"] + KV_WIDTH
    return pl.pallas_call(
        _attn_kernel,
        out_shape=jax.ShapeDtypeStruct((bsz * t, ATTN_WIDTH), jnp.bfloat16),
        grid=(bsz, nb),
        in_specs=[pl.BlockSpec(memory_space=pltpu.SMEM),
                  pl.BlockSpec((w, ATTN_WIDTH), lambda b, n: (b * nb + n, q_blk)),
                  pl.BlockSpec((w, 2 * KV_WIDTH), lambda b, n: (b * nb + jnp.maximum(n - 1, 0), kv_blk)),
                  pl.BlockSpec((w, 2 * KV_WIDTH), lambda b, n: (b * nb + n, kv_blk)),
                  pl.BlockSpec((w, ATTN_WIDTH), lambda b, n: (b * nb + n, g_blk)),
                  pl.BlockSpec((None, ATTN_HEADS, w, 2 * w),
                               lambda b, n: (jnp.where(n == 0, 1, 0), 0, 0, 0))],
        out_specs=pl.BlockSpec((w, ATTN_WIDTH), lambda b, n: (b * nb + n, 0)),
        compiler_params=pltpu.CompilerParams(
            dimension_semantics=("parallel", "arbitrary"), vmem_limit_bytes=VMEM_LIMIT_BYTES),
        name="attn",
    )(sinks_l.astype(jnp.float32), u, u, u, u, bias)


def _tail_kernel(za_ref, zb_ref, cb_ref, cc_ref, cx_ref, cg_ref, hc_ref, hx_ref,
                 ma_ref, mb_ref, mc_ref, x_ref, wa_ref, wb_ref, wc_ref, wo_ref,
                 cw_ref, lng_ref, lnb_ref, xo_ref, xb_ref, *, tiles_per_seq):
    bf = jnp.bfloat16
    tm = x_ref.shape[0]
    h = cc_ref[...] * cx_ref[...]
    seq_start = (pl.program_id(0) % tiles_per_seq) == 0
    halo = jnp.where(seq_start, 0.0, hc_ref[...] * hx_ref[...])
    row = lax.broadcasted_iota(jnp.int32, (tm, CONV_WIDTH), 0)
    h1 = jnp.where(row >= 1, pltpu.roll(h, 1, 0), halo[7:8, :])
    h2 = jnp.where(row >= 2, pltpu.roll(h, 2, 0),
                   jnp.where(row == 1, halo[7:8, :], halo[6:7, :]))
    cw = cw_ref[...]
    y = cw[0:1, :] * h2 + cw[1:2, :] * h1 + cw[2:3, :] * h
    zc = (cb_ref[...] * y * _silu(cg_ref[...])).astype(bf)
    za, zb = za_ref[...], zb_ref[...]
    yo = None
    for c0 in range(0, D_MODEL, TAIL_MERGE_CHUNK):
        cs = slice(c0, c0 + TAIL_MERGE_CHUNK)
        ya = jnp.dot(za, wa_ref[:, cs], preferred_element_type=jnp.float32)
        yb = jnp.dot(zb, wb_ref[:, cs], preferred_element_type=jnp.float32)
        yc = jnp.dot(zc, wc_ref[:, cs], preferred_element_type=jnp.float32)
        merged = (_sigmoid(ma_ref[:, cs]) * ya + _sigmoid(mb_ref[:, cs]) * yb
                  + _sigmoid(mc_ref[:, cs]) * yc)
        part = jnp.dot(merged.astype(bf), wo_ref[cs, :], preferred_element_type=jnp.float32)
        yo = part if yo is None else yo + part
    r = ALPHA * x_ref[...] + yo
    mu = jnp.mean(r, axis=-1, keepdims=True)
    rc = r - mu
    var = jnp.mean(rc * rc, axis=-1, keepdims=True)
    xn = rc * lax.rsqrt(var + LN_EPS) * lng_ref[...] + lnb_ref[...]
    xo_ref[...] = xn
    xb_ref[...] = xn.astype(bf)


def _tail(za, zb, u, x, wa, wb, wc, wo, conv_w_l, ln_g_l, ln_b_l, t, layer):
    m = x.shape[0]
    tm = min(256, t)
    assert t % tm == 0
    tiles_per_seq = t // tm
    cw = CONV_WIDTH

    def seg(name, width):
        blk = _OUR_OFF[name] // width
        return pl.BlockSpec((tm, width), lambda i: (i, blk))

    def halo(name):
        blk = _OUR_OFF[name] // cw
        return pl.BlockSpec((8, cw), lambda i: (jnp.maximum(i * (tm // 8) - 1, 0), blk))

    def const(shape):
        return pl.BlockSpec(shape, lambda i: (0,) * len(shape), pipeline_mode=pl.Buffered(1))

    def weight(rows):
        return pl.BlockSpec((None, rows, D_MODEL), lambda i: (layer, 0, 0),
                            pipeline_mode=pl.Buffered(1))

    return pl.pallas_call(
        functools.partial(_tail_kernel, tiles_per_seq=tiles_per_seq),
        out_shape=(jax.ShapeDtypeStruct((m, D_MODEL), jnp.float32),
                   jax.ShapeDtypeStruct((m, D_MODEL), jnp.bfloat16)),
        grid=(m // tm,),
        in_specs=[pl.BlockSpec((tm, HGRN_WIDTH), lambda i: (i, 0)),
                  pl.BlockSpec((tm, ATTN_WIDTH), lambda i: (i, 0)),
                  seg("c_b", cw), seg("c_c", cw), seg("c_x", cw), seg("c_g", cw),
                  halo("c_c"), halo("c_x"),
                  seg("m_a", D_MODEL), seg("m_b", D_MODEL), seg("m_c", D_MODEL),
                  pl.BlockSpec((tm, D_MODEL), lambda i: (i, 0)),
                  weight(HGRN_WIDTH), weight(ATTN_WIDTH), weight(CONV_WIDTH), weight(D_MODEL),
                  const((CONV_K, cw)), const((1, D_MODEL)), const((1, D_MODEL))],
        out_specs=(pl.BlockSpec((tm, D_MODEL), lambda i: (i, 0)),
                   pl.BlockSpec((tm, D_MODEL), lambda i: (i, 0))),
        compiler_params=pltpu.CompilerParams(
            dimension_semantics=("parallel",), vmem_limit_bytes=VMEM_LIMIT_BYTES),
        name="tail",
    )(za, zb, u, u, u, u, u, u, u, u, u, x, wa, wb, wc, wo,
      conv_w_l, ln_g_l.reshape(1, D_MODEL), ln_b_l.reshape(1, D_MODEL))


def kernel(x, w_in, w_proj_hgrn, w_proj_attn, w_proj_conv, w_out, lb_param, hgrn_norm_g,
           attn_sinks, conv_w, rel_bias, ln_g, ln_b):
    bsz, t, d = x.shape
    assert d == D_MODEL and t % WINDOW == 0
    bf = jnp.bfloat16
    lower, bias = _prep(lb_param, rel_bias)
    w_in = w_in.astype(jnp.float32)
    wa_b, wb_b, wc_b, wo_b = (w.astype(bf) for w in (w_proj_hgrn, w_proj_attn, w_proj_conv, w_out))
    xf = x.reshape(bsz * t, d).astype(jnp.float32)
    xb = xf.astype(bf)
    for l in range(DEPTH):
        u = _inproj(xb, w_in, l)
        za = _hgrn(u, lower[l], hgrn_norm_g[l].astype(jnp.float32), bsz, t)
        zb = _attn(u, bias, attn_sinks[l], bsz, t)
        xf, xb = _tail(za, zb, u, xf, wa_b, wb_b, wc_b, wo_b,
                       conv_w[l].astype(jnp.float32), ln_g[l].astype(jnp.float32),
                       ln_b[l].astype(jnp.float32), t, l)
    return xf.reshape(bsz, t, d).astype(x.dtype)
```
